```python
import math
import jax, jax.numpy as jnp
from jax import lax
import numpy as np

D_MODEL = 2048
BATCH = 4
SEQ = 2048
DEPTH = 4
DEC_BATCH = 128
DEC_SEQ = 8
PAST_LEN = 16384
PAGE_SIZE = 128

D_A = D_MODEL // 2
A_GROUPS = 8
A_GW = D_A // A_GROUPS
CHUNK = 128
D_B = D_MODEL // 2
B_DK = 128
B_HEADS = D_B // B_DK
B_DV = D_B // B_HEADS
HGRN_CHUNK = 64
IN_SIZES = (D_A, D_A, D_B, D_B, D_B, D_B, D_MODEL, D_MODEL)
IN_W = sum(IN_SIZES)
IN_SPLITS = tuple(int(s) for s in np.cumsum(IN_SIZES)[:-1])
N_EXPERTS = 32
TOP_K = 4
D_FF = D_MODEL
SWIGLU_LIMIT = 7.0
SWIGLU_ALPHA = 1.702
MOE_BLOCK = 128
N_MOD = 6
EPS = 1e-6

kernel_name = 'hybrid_gmlp_hgrn2_moe_adaln_step'


def _rms_norm(x, g):
    xf = x.astype(jnp.float32)
    y = xf * lax.rsqrt(jnp.mean(xf * xf, axis=-1, keepdims=True) + EPS)
    return (y * g.astype(jnp.float32)).astype(x.dtype)


def _layer_norm(x, g, b):
    xf = x.astype(jnp.float32)
    xc = xf - jnp.mean(xf, axis=-1, keepdims=True)
    y = xc * lax.rsqrt(jnp.mean(xc * xc, axis=-1, keepdims=True) + EPS)
    return (y * g.astype(jnp.float32) + b.astype(jnp.float32)).astype(x.dtype)


def _chunk_gmlp(u, vn, w_s, b_s):
    n, t, _ = u.shape
    L = min(t, CHUNK)
    nc = t // L
    causal = jnp.tril(jnp.ones((L, L), dtype=bool))
    ws = jnp.where(causal[None], w_s[:, :L, :L], jnp.zeros((), w_s.dtype))
    vr = vn.reshape(n, nc, L, A_GROUPS, A_GW)
    mixed = jnp.einsum('gts,ncsgd->nctgd', ws, vr) + b_s[:, :L].T[None, None, :, :, None]
    return (u.reshape(n, nc, L, A_GROUPS, A_GW) * mixed).reshape(n, t, D_A)


def _hgrn2_recurrence(q, k, logf, v, s0):
    n, t, h, _ = q.shape
    dv = v.shape[-1]
    c = math.gcd(t, HGRN_CHUNK)
    nc = t // c

    def to_chunks(a):
        return a.reshape(n, nc, c, h, a.shape[-1]).transpose(1, 0, 2, 3, 4)

    causal = jnp.tril(jnp.ones((c, c), dtype=bool))[None, :, :, None, None]

    def step(s, inp):
        qc, kc, lfc, vc = inp
        g = jnp.cumsum(lfc, axis=1)
        o = jnp.einsum('nthk,nhkv->nthv', qc * jnp.exp(g), s)
        rel = jnp.exp(jnp.where(causal, g[:, :, None] - g[:, None, :], -jnp.inf))
        a = jnp.sum(qc[:, :, None] * kc[:, None] * rel, axis=-1)
        o = o + jnp.einsum('ntsh,nshv->nthv', a, vc)
        g_last = g[:, -1]
        s = jnp.exp(g_last)[..., None] * s + jnp.einsum('nshk,nshv->nhkv', kc * jnp.exp(g_last[:, None] - g), vc)
        return s, o

    s_final, o = lax.scan(step, s0, (to_chunks(q), to_chunks(k), to_chunks(logf), to_chunks(v)))
    return o.transpose(1, 0, 2, 3, 4).reshape(n, t, h, dv), s_final


def _token_mixers(h, s0, lb, w_in, ln_g, ln_b, w_s, b_s, hn_g, w_a, w_b, w_o):
    n, t, _ = h.shape
    f32 = jnp.float32
    u, v, q, f, i, g, gate_a, gate_b = jnp.split(h @ w_in, IN_SPLITS, axis=-1)
    u = jax.nn.gelu(u, approximate=False)
    vn = _layer_norm(jax.nn.gelu(v, approximate=False), ln_g, ln_b)
    y_a = _chunk_gmlp(u, vn, w_s, b_s)
    lbh = lb.reshape(B_HEADS, B_DK)
    zf = f.astype(f32).reshape(n, t, B_HEADS, B_DK)
    logf = jnp.logaddexp(jnp.log(lbh), jnp.log1p(-lbh) + jax.nn.log_sigmoid(zf))
    k = (1.0 - lbh) * jax.nn.sigmoid(-zf)
    qh = jax.nn.silu(q.astype(f32)).reshape(n, t, B_HEADS, B_DK)
    vh = i.astype(f32).reshape(n, t, B_HEADS, B_DV)
    o, s_new = _hgrn2_recurrence(qh, k, logf, vh, s0.astype(f32))
    o = o * lax.rsqrt(jnp.mean(o * o, axis=-1, keepdims=True) + EPS)
    y_b = (o.reshape(n, t, D_B) * hn_g.astype(f32)).astype(h.dtype) * jax.nn.silu(g)
    merged = jax.nn.sigmoid(gate_a) * (y_a @ w_a) + jax.nn.sigmoid(gate_b) * (y_b @ w_b)
    return merged @ w_o, s_new.astype(s0.dtype), vn


def _moe(x, l, router_w, router_b, w_up, b_up, w_down, b_down):
    t, d = x.shape
    logits = (x @ router_w[l]).astype(jnp.float32) + router_b[l].astype(jnp.float32)
    top_logit, top_e = lax.top_k(logits, TOP_K)
    weights = jax.nn.softmax(top_logit, axis=-1).astype(x.dtype)
    tk = t * TOP_K
    slot_e = top_e.reshape(tk)
    slot_tok = jnp.arange(tk, dtype=jnp.int32) // TOP_K
    order = jnp.argsort(slot_e)
    e_sorted = slot_e[order]
    counts = jnp.bincount(slot_e, length=N_EXPERTS)
    padded = (counts + MOE_BLOCK - 1) // MOE_BLOCK * MOE_BLOCK
    pad_end = jnp.cumsum(padded)
    pad_start = pad_end - padded
    start = jnp.cumsum(counts) - counts
    dest = pad_start[e_sorted] + jnp.arange(tk, dtype=jnp.int32) - start[e_sorted]
    n_blocks = -(-tk // MOE_BLOCK) + N_EXPERTS
    row_tok = jnp.full((n_blocks * MOE_BLOCK,), t, dtype=jnp.int32).at[dest].set(slot_tok[order])
    block_e = jnp.minimum(jnp.searchsorted(pad_end, jnp.arange(n_blocks, dtype=jnp.int32) * MOE_BLOCK, side='right'), N_EXPERTS - 1)
    x_pad = jnp.concatenate([x, jnp.zeros((1, d), x.dtype)], axis=0)
    xb = x_pad[row_tok].reshape(n_blocks, MOE_BLOCK, d)

    def expert_block(args):
        rows, e = args
        hu = rows @ w_up[l, e] + b_up[l, e]
        gate = jnp.minimum(hu[:, :D_FF], SWIGLU_LIMIT)
        lin = jnp.clip(hu[:, D_FF:], -SWIGLU_LIMIT, SWIGLU_LIMIT)
        glu = gate * jax.nn.sigmoid(SWIGLU_ALPHA * gate)
        return ((lin + 1.0) * glu) @ w_down[l, e] + b_down[l, e]

    yb = lax.map(expert_block, (xb, block_e))
    y_slots = yb.reshape(n_blocks * MOE_BLOCK, d)[dest]
    w_sorted = weights.reshape(tk)[order]
    return jax.ops.segment_sum(y_slots * w_sorted[:, None], slot_tok[order], num_segments=t)


def _trunk(x, c, s_init, w_mod, b_mod, norm1_g, w_in, gmlp_ln_g, gmlp_ln_b, gmlp_w_s, gmlp_b_s,
           hgrn_lb_raw, hgrn_norm_g, w_branch_a, w_branch_b, w_out, norm2_g, router_w, router_b,
           exp_w_up, exp_b_up, exp_w_down, exp_b_down, final_norm_g):
    n, t, d = x.shape
    p = jax.nn.softmax(hgrn_lb_raw.astype(jnp.float32), axis=0)
    lower_bounds = jnp.maximum(jnp.cumsum(p, axis=0) - p[0:1], 0.0)
    c_act = jax.nn.silu(c)
    states, v_rows = [], []
    for l in range(DEPTH):
        mod = (c_act @ w_mod[l] + b_mod[l])[:, None, :]
        shift1, scale1, gate1, shift2, scale2, gate2 = jnp.split(mod, N_MOD, axis=-1)
        h = _rms_norm(x, norm1_g[l]) * (1 + scale1) + shift1
        mix, s_new, vn = _token_mixers(h, s_init[l], lower_bounds[l], w_in[l], gmlp_ln_g[l], gmlp_ln_b[l],
                                       gmlp_w_s[l], gmlp_b_s[l], hgrn_norm_g[l], w_branch_a[l],
                                       w_branch_b[l], w_out[l])
        x = x + gate1 * mix
        h = _rms_norm(x, norm2_g[l]) * (1 + scale2) + shift2
        y = _moe(h.reshape(n * t, d), l, router_w, router_b, exp_w_up, exp_b_up, exp_w_down, exp_b_down)
        x = x + gate2 * y.reshape(n, t, d)
        states.append(s_new)
        v_rows.append(vn)
    return _rms_norm(x, final_norm_g), jnp.stack(states), jnp.stack(v_rows)


def setup_inputs(seed: int = 0) -> dict:
    key = jax.random.key(seed)
    ks = jax.random.split(key, 32)

    def nrm(k, shape, s):
        return jax.random.normal(k, shape, jnp.float32) * s

    D = D_MODEL
    return {
        'x_prompt': nrm(ks[0], (BATCH, SEQ, D), 1.0),
        'x_sample': nrm(ks[1], (DEC_BATCH, DEC_SEQ, D), 1.0),
        'c_prompt': nrm(ks[2], (BATCH, D), 1.0),
        'c_sample': nrm(ks[3], (DEC_BATCH, D), 1.0),
        'state_hgrn': nrm(ks[4], (DEPTH, DEC_BATCH, B_HEADS, B_DK, B_DV), 0.5),
        'w_mod': nrm(ks[5], (DEPTH, D, N_MOD * D), 0.3 * D ** -0.5),
        'b_mod': nrm(ks[6], (DEPTH, N_MOD * D), 0.02),
        'norm1_g': 1.0 + nrm(ks[7], (DEPTH, D), 0.02),
        'w_in': nrm(ks[8], (DEPTH, D, IN_W), D ** -0.5),
        'gmlp_ln_g': 1.0 + nrm(ks[9], (DEPTH, D_A), 0.02),
        'gmlp_ln_b': nrm(ks[10], (DEPTH, D_A), 0.02),
        'gmlp_w_s': nrm(ks[11], (DEPTH, A_GROUPS, CHUNK, CHUNK), CHUNK ** -0.5),
        'gmlp_b_s': 1.0 + nrm(ks[12], (DEPTH, A_GROUPS, CHUNK), 0.02),
        'hgrn_lb_raw': 1.0 + nrm(ks[13], (DEPTH, D_B), 0.1),
        'hgrn_norm_g': 1.0 + nrm(ks[14], (DEPTH, D_B), 0.02),
        'w_branch_a': nrm(ks[15], (DEPTH, D_A, D), D_A ** -0.5),
        'w_branch_b': nrm(ks[16], (DEPTH, D_B, D), D_B ** -0.5),
        'w_out': nrm(ks[17], (DEPTH, D, D), D ** -0.5),
        'norm2_g': 1.0 + nrm(ks[18], (DEPTH, D), 0.02),
        'router_w': nrm(ks[19], (DEPTH, D, N_EXPERTS), D ** -0.5),
        'router_b': nrm(ks[20], (DEPTH, N_EXPERTS), 0.01),
        'exp_w_up': nrm(ks[21], (DEPTH, N_EXPERTS, D, 2 * D_FF), D ** -0.5),
        'exp_b_up': nrm(ks[22], (DEPTH, N_EXPERTS, 2 * D_FF), 0.01),
        'exp_w_down': nrm(ks[23], (DEPTH, N_EXPERTS, D_FF, D), D_FF ** -0.5),
        'exp_b_down': nrm(ks[24], (DEPTH, N_EXPERTS, D), 0.01),
        'final_norm_g': 1.0 + nrm(ks[25], (D,), 0.02),
    }


def reference(x_prompt, x_sample, c_prompt, c_sample, state_hgrn, w_mod, b_mod, norm1_g, w_in,
              gmlp_ln_g, gmlp_ln_b, gmlp_w_s, gmlp_b_s, hgrn_lb_raw, hgrn_norm_g, w_branch_a,
              w_branch_b, w_out, norm2_g, router_w, router_b, exp_w_up, exp_b_up, exp_w_down,
              exp_b_down, final_norm_g):
    params = (w_mod, b_mod, norm1_g, w_in, gmlp_ln_g, gmlp_ln_b, gmlp_w_s, gmlp_b_s, hgrn_lb_raw,
              hgrn_norm_g, w_branch_a, w_branch_b, w_out, norm2_g, router_w, router_b, exp_w_up,
              exp_b_up, exp_w_down, exp_b_down, final_norm_g)
    s0_prompt = jnp.zeros((DEPTH, x_prompt.shape[0], B_HEADS, B_DK, B_DV), state_hgrn.dtype)
    y_prompt, state_hgrn_prompt, _ = _trunk(x_prompt, c_prompt, s0_prompt, *params)
    y_sample, state_hgrn_sample, state_gmlp_v_sample = _trunk(x_sample, c_sample, state_hgrn, *params)
    return (y_prompt, y_sample, state_hgrn_prompt, state_hgrn_sample, state_gmlp_v_sample)
```

```python
import functools

import jax
import jax.numpy as jnp
from jax import lax
from jax.experimental import pallas as pl
from jax.experimental.pallas import tpu as pltpu

f32 = jnp.float32
bf16 = jnp.bfloat16

D_MODEL = 2048
DEPTH = 4
D_A = D_MODEL // 2
A_GROUPS = 8
A_GW = D_A // A_GROUPS
CHUNK = 128
D_B = D_MODEL // 2
B_DK = 128
B_HEADS = D_B // B_DK
IN_W = 2 * D_A + 4 * D_B + 2 * D_MODEL
N_EXPERTS = 32
TOP_K = 4
D_FF = D_MODEL
SWIGLU_LIMIT = 7.0
SWIGLU_ALPHA = 1.702
N_MOD = 6
EPS = 1e-6

VMEM_LIMIT_BYTES = 56 * 1024 * 1024

ROW_BLOCK = 256
IN_TM = 1024
IN_TN = 1024
MOE_TM = 256
UP_TF = 512
DOWN_TN = 1024
HGRN_TB = 256
HGRN_C = 16
HGRN_NS = 8


def _params(sem):
    return pltpu.CompilerParams(dimension_semantics=sem, vmem_limit_bytes=VMEM_LIMIT_BYTES)


def _sigmoid(x):
    return jax.nn.sigmoid(x)


def _gelu(x):
    return 0.5 * x * (1.0 + lax.erf(x * 0.7071067811865476))


def _mod_kernel(c_ref, w_ref, b_ref, o_ref):
    c = c_ref[...]
    a = (c * _sigmoid(c)).astype(bf16)
    o_ref[...] = jnp.dot(a, w_ref[...].astype(bf16), preferred_element_type=f32) + b_ref[...]


def _modulation(c_all, w_mod, b_mod):
    n = c_all.shape[0]
    tn = 1024
    return pl.pallas_call(
        _mod_kernel,
        grid=(DEPTH, N_MOD * D_MODEL // tn),
        in_specs=[
            pl.BlockSpec((n, D_MODEL), lambda l, j: (0, 0)),
            pl.BlockSpec((None, D_MODEL, tn), lambda l, j: (l, 0, j)),
            pl.BlockSpec((None, 1, tn), lambda l, j: (l, 0, j)),
        ],
        out_specs=pl.BlockSpec((None, n, tn), lambda l, j: (l, 0, j)),
        out_shape=jax.ShapeDtypeStruct((DEPTH, n, N_MOD * D_MODEL), f32),
        compiler_params=_params(("arbitrary", "arbitrary")),
        name="modulation",
    )(c_all, w_mod, b_mod.reshape(DEPTH, 1, N_MOD * D_MODEL))


def _norm_kernel(has_res, has_mod, write_x, *refs):
    refs = list(refs)
    x_ref = refs.pop(0)
    if has_res:
        y_ref, gate_ref = refs.pop(0), refs.pop(0)
    g_ref = refs.pop(0)
    if has_mod:
        shift_ref, scale_ref = refs.pop(0), refs.pop(0)
    if write_x:
        xo_ref = refs.pop(0)
    h_ref = refs.pop(0)

    x = x_ref[...]
    nb, tb, d = x.shape
    if has_res:
        x = x + gate_ref[...] * y_ref[...].reshape(nb, tb, d)
    if write_x:
        xo_ref[...] = x
    hn = x * lax.rsqrt(jnp.mean(x * x, axis=-1, keepdims=True) + EPS) * g_ref[...]
    if has_mod:
        hn = hn * (1.0 + scale_ref[...]) + shift_ref[...]
    h_ref[...] = hn.reshape(nb * tb, d).astype(h_ref.dtype)


def _norm(x3, g, *, y=None, y_row0=0, gate=None, gate_col=0, mod=None, shift_col=0, scale_col=0,
          write_x=False, out_dtype=bf16):
    n, t, d = x3.shape
    tb = min(t, ROW_BLOCK)
    nb = ROW_BLOCK // tb
    rows = nb * tb
    grid = (n // nb, t // tb)
    tpb = t // tb
    x_spec = pl.BlockSpec((nb, tb, d), lambda i, j: (i, j, 0))

    def vec_spec(col):
        return pl.BlockSpec((nb, 1, d), lambda i, j: (i, 0, col))

    ins, specs = [x3], [x_spec]
    if y is not None:
        base = y_row0 // rows
        ins += [y, gate]
        specs += [pl.BlockSpec((rows, d), lambda i, j: (base + i * tpb + j, 0)), vec_spec(gate_col)]
    ins.append(g.reshape(1, 1, d))
    specs.append(pl.BlockSpec((1, 1, d), lambda i, j: (0, 0, 0)))
    if mod is not None:
        ins += [mod, mod]
        specs += [vec_spec(shift_col), vec_spec(scale_col)]
    out_shapes, out_specs = [], []
    if write_x:
        out_shapes.append(jax.ShapeDtypeStruct((n, t, d), f32))
        out_specs.append(x_spec)
    out_shapes.append(jax.ShapeDtypeStruct((n * t, d), out_dtype))
    out_specs.append(pl.BlockSpec((rows, d), lambda i, j: (i * tpb + j, 0)))
    res = pl.pallas_call(
        functools.partial(_norm_kernel, y is not None, mod is not None, write_x),
        grid=grid, in_specs=specs, out_specs=out_specs, out_shape=out_shapes,
        compiler_params=_params(("arbitrary", "arbitrary")),
        name="norm",
    )(*ins)
    return res if write_x else (None, res[0])


def _in_kernel(n_p, hp_ref, hs_ref, w_ref, o_ref, wb_ref):
    m = pl.program_id(1)

    @pl.when(m == 0)
    def _():
        wb_ref[...] = w_ref[...].astype(bf16)

    @pl.when(m < n_p)
    def _():
        o_ref[...] = jnp.dot(hp_ref[...], wb_ref[...], preferred_element_type=f32).astype(bf16)

    @pl.when(m >= n_p)
    def _():
        o_ref[...] = jnp.dot(hs_ref[...], wb_ref[...], preferred_element_type=f32).astype(bf16)


def _in_proj(h_p, h_s, w_in, layer):
    rows_p, rows_s = h_p.shape[0], h_s.shape[0]
    n_p, n_s = rows_p // IN_TM, rows_s // IN_TM
    return pl.pallas_call(
        functools.partial(_in_kernel, n_p),
        grid=(IN_W // IN_TN, n_p + n_s),
        in_specs=[
            pl.BlockSpec((IN_TM, D_MODEL), lambda j, m: (jnp.minimum(m, n_p - 1), 0)),
            pl.BlockSpec((IN_TM, D_MODEL), lambda j, m: (jnp.maximum(m - n_p, 0), 0)),
            pl.BlockSpec((None, D_MODEL, IN_TN), lambda j, m: (layer, 0, j)),
        ],
        out_specs=pl.BlockSpec((IN_TM, IN_TN), lambda j, m: (m, j)),
        out_shape=jax.ShapeDtypeStruct((rows_p + rows_s, IN_W), bf16),
        scratch_shapes=[pltpu.VMEM((D_MODEL, IN_TN), bf16)],
        compiler_params=_params(("arbitrary", "arbitrary")),
        name="in_proj",
    )(h_p, h_s, w_in)


def _gmlp_kernel(n_prompt_chunks, sample_shift, u_ref, v_ref, w_ref, b_ref, lg_ref, lb_ref, ya_ref, vn_ref):
    c = pl.program_id(0)
    u = _gelu(u_ref[...].astype(f32))
    v = _gelu(v_ref[...].astype(f32))
    vc = v - jnp.mean(v, axis=-1, keepdims=True)
    vn = vc * lax.rsqrt(jnp.mean(vc * vc, axis=-1, keepdims=True) + EPS) * lg_ref[...] + lb_ref[...]
    vn_ref[...] = vn
    shift = jnp.where(c >= n_prompt_chunks, sample_shift, 7)
    row = lax.broadcasted_iota(jnp.int32, (CHUNK, CHUNK), 0)
    col = lax.broadcasted_iota(jnp.int32, (CHUNK, CHUNK), 1)
    mask = ((row >> shift) == (col >> shift)) & (col <= row)
    vnb = vn.astype(bf16)
    mixed = []
    for g in range(A_GROUPS):
        wg = jnp.where(mask, w_ref[g], 0.0).astype(bf16)
        mixed.append(jnp.dot(wg, vnb[:, g * A_GW:(g + 1) * A_GW], preferred_element_type=f32))
    mixed = jnp.concatenate(mixed, axis=1) + b_ref[...]
    ya_ref[...] = (u * mixed).astype(bf16)


def _gmlp(z, w_mix, b_mix, ln_g, ln_b, rows_p, t_s):
    rows = z.shape[0]
    ncp = rows_p // CHUNK
    return pl.pallas_call(
        functools.partial(_gmlp_kernel, ncp, t_s.bit_length() - 1),
        grid=(rows // CHUNK,),
        in_specs=[
            pl.BlockSpec((CHUNK, D_A), lambda c: (c, 0)),
            pl.BlockSpec((CHUNK, D_A), lambda c: (c, 1)),
            pl.BlockSpec((None, A_GROUPS, CHUNK, CHUNK), lambda c: (jnp.where(c >= ncp, 1, 0), 0, 0, 0)),
            pl.BlockSpec((None, CHUNK, D_A), lambda c: (jnp.where(c >= ncp, 1, 0), 0, 0)),
            pl.BlockSpec((1, D_A), lambda c: (0, 0)),
            pl.BlockSpec((1, D_A), lambda c: (0, 0)),
        ],
        out_specs=[pl.BlockSpec((CHUNK, D_A), lambda c: (c, 0)), pl.BlockSpec((CHUNK, D_A), lambda c: (c, 0))],
        out_shape=[jax.ShapeDtypeStruct((rows, D_A), bf16), jax.ShapeDtypeStruct((rows, D_A), f32)],
        compiler_params=_params(("arbitrary",)),
        name="gmlp",
    )(z, z, w_mix, b_mix, ln_g, ln_b)


def _hgrn_kernel(ns, tb, c, q_ref, f_ref, i_ref, g_ref, s0_ref, llb_ref, l1m_ref, oml_ref, hng_ref,
                 yb_ref, s_ref):
    @pl.when(pl.program_id(1) == 0)
    def _():
        s_ref[...] = s0_ref[...]

    row = lax.broadcasted_iota(jnp.int32, (c, c), 0)
    col = lax.broadcasted_iota(jnp.int32, (c, c), 1)
    tri = (col <= row).astype(f32)
    rows1 = lax.broadcasted_iota(jnp.int32, (c, 1), 0)
    llb, l1m, oml, hng = llb_ref[...], l1m_ref[...], oml_ref[...], hng_ref[...]

    def chunk(si, r0):
        zq = q_ref[pl.ds(r0, c), :].astype(f32)
        zf = f_ref[pl.ds(r0, c), :].astype(f32)
        zi = i_ref[pl.ds(r0, c), :].astype(f32)
        zg = g_ref[pl.ds(r0, c), :].astype(f32)
        qs = zq * _sigmoid(zq)
        a = l1m + jnp.minimum(zf, 0.0) - jnp.log1p(jnp.exp(-jnp.abs(zf)))
        logf = jnp.maximum(a, llb) + jnp.log1p(jnp.exp(-jnp.abs(a - llb)))
        kk = oml * _sigmoid(-zf)
        gc = jnp.dot(tri, logf, preferred_element_type=f32, precision=lax.Precision.HIGHEST)
        g_last = gc[c - 1:c, :]
        q_dec = (qs * jnp.exp(gc)).astype(bf16)
        k_dec = (kk * jnp.exp(g_last - gc)).astype(bf16)
        vb = zi.astype(bf16)
        dec = jnp.exp(jnp.concatenate([g_last[:, h * B_DK:(h + 1) * B_DK] for h in range(B_HEADS)], axis=0))
        dec_t = dec.T
        gate = zg * _sigmoid(zg)
        for h in range(B_HEADS):
            sl = slice(h * B_DK, (h + 1) * B_DK)
            s_prev = s_ref[si, h]
            o = jnp.dot(q_dec[:, sl], s_prev.astype(bf16), preferred_element_type=f32)
            qh, gh, kh, vh = qs[:, sl], gc[:, sl], kk[:, sl], zi[:, sl]
            for s in range(c):
                e = jnp.exp(jnp.minimum(gh - gh[s:s + 1, :], 0.0))
                a_s = jnp.sum(qh * (kh[s:s + 1, :] * e), axis=-1, keepdims=True)
                o = o + jnp.where(rows1 >= s, a_s, 0.0) * vh[s:s + 1, :]
            upd = lax.dot_general(k_dec[:, sl], vb[:, sl], (((0,), (0,)), ((), ())),
                                  preferred_element_type=f32)
            s_ref[si, h] = dec_t[:, h:h + 1] * s_prev + upd
            on = o * lax.rsqrt(jnp.mean(o * o, axis=-1, keepdims=True) + EPS) * hng[:, sl]
            yb_ref[pl.ds(r0, c), sl] = (on * gate[:, sl]).astype(bf16)

    n_chunks = tb // c
    for si in range(ns):
        if n_chunks == 1:
            chunk(si, si * tb)
        else:
            def body(ci, carry, si=si):
                chunk(si, pl.multiple_of(si * tb + ci * c, c))
                return carry
            lax.fori_loop(0, n_chunks, body, 0)


def _hgrn(z, s0, layer, vecs, n_seq, t, row0, ns, tb, c):
    rows = ns * tb
    tpb = t // tb
    base = row0 // rows
    grid = (n_seq // ns, tpb)

    def zspec(colblk):
        return pl.BlockSpec((rows, D_B), lambda n, j: (base + n * tpb + j, colblk))

    vspec = pl.BlockSpec((1, D_B), lambda n, j: (0, 0))
    sspec = pl.BlockSpec((ns, B_HEADS, B_DK, B_DK), lambda n, j: (n, 0, 0, 0))
    yb, s_new = pl.pallas_call(
        functools.partial(_hgrn_kernel, ns, tb, c),
        grid=grid,
        in_specs=[zspec(2), zspec(3), zspec(4), zspec(5),
                  pl.BlockSpec((None, ns, B_HEADS, B_DK, B_DK), lambda n, j: (layer, n, 0, 0, 0)),
                  vspec, vspec, vspec, vspec],
        out_specs=[pl.BlockSpec((rows, D_B), lambda n, j: (n * tpb + j, 0)), sspec],
        out_shape=[jax.ShapeDtypeStruct((n_seq * t, D_B), bf16),
                   jax.ShapeDtypeStruct((n_seq, B_HEADS, B_DK, B_DK), f32)],
        compiler_params=_params(("arbitrary", "arbitrary")),
        name="hgrn",
    )(z, z, z, z, s0, *vecs)
    return yb, s_new


def _merge_kernel(ya_ref, yb_ref, ga_ref, gb_ref, x_ref, mg_ref, wa_ref, wb_ref, wo_ref, n2_ref,
                  sh_ref, sc_ref, rw_ref, rb_ref, xo_ref, h2_ref, lg_ref):
    a = jnp.dot(ya_ref[...], wa_ref[...], preferred_element_type=f32)
    b = jnp.dot(yb_ref[...], wb_ref[...], preferred_element_type=f32)
    merged = _sigmoid(ga_ref[...].astype(f32)) * a + _sigmoid(gb_ref[...].astype(f32)) * b
    mix = jnp.dot(merged.astype(bf16), wo_ref[...], preferred_element_type=f32)
    x = x_ref[...]
    nb, tb, d = x.shape
    x = x + mg_ref[...] * mix.reshape(nb, tb, d)
    xo_ref[...] = x
    hn = x * lax.rsqrt(jnp.mean(x * x, axis=-1, keepdims=True) + EPS) * n2_ref[...]
    hn = (hn * (1.0 + sc_ref[...]) + sh_ref[...]).reshape(nb * tb, d)
    h2_ref[...] = hn.astype(bf16)
    lg_ref[...] = jnp.dot(hn, rw_ref[...], preferred_element_type=f32,
                          precision=lax.Precision.HIGHEST) + rb_ref[...]


def _merge(ya, yb_g, z, x3, mod, wa, wb, wo, n2g, rw, rb, row0):
    n, t, d = x3.shape
    tb = min(t, ROW_BLOCK)
    nb = ROW_BLOCK // tb
    rows = nb * tb
    tpb = t // tb
    base = row0 // rows
    grid = (n // nb, tpb)
    ga_blk = (2 * D_A + 4 * D_B) // D_MODEL

    def rspec(width, colblk, b):
        return pl.BlockSpec((rows, width), lambda i, j: (b + i * tpb + j, colblk))

    def vec_spec(col):
        return pl.BlockSpec((nb, 1, d), lambda i, j: (i, 0, col))

    def const(shape):
        return pl.BlockSpec(shape, lambda i, j: tuple(0 for _ in shape), pipeline_mode=pl.Buffered(1))

    x_spec = pl.BlockSpec((nb, tb, d), lambda i, j: (i, j, 0))
    return pl.pallas_call(
        _merge_kernel,
        grid=grid,
        in_specs=[rspec(D_A, 0, base), rspec(D_B, 0, 0), rspec(d, ga_blk, base), rspec(d, ga_blk + 1, base),
                  x_spec, vec_spec(2),
                  const((D_A, d)), const((D_B, d)), const((d, d)), const((1, 1, d)),
                  vec_spec(3), vec_spec(4), const((d, N_EXPERTS)), const((1, N_EXPERTS))],
        out_specs=[x_spec, rspec(d, 0, 0), rspec(N_EXPERTS, 0, 0)],
        out_shape=[jax.ShapeDtypeStruct((n, t, d), f32), jax.ShapeDtypeStruct((n * t, d), bf16),
                   jax.ShapeDtypeStruct((n * t, N_EXPERTS), f32)],
        compiler_params=_params(("arbitrary", "arbitrary")),
        name="merge",
    )(ya, yb_g, z, z, x3, mod, wa, wb, wo, n2g.reshape(1, 1, d), mod, mod, rw, rb.reshape(1, N_EXPERTS))


def _up_kernel(be_ref, first_ref, nv_ref, xs_ref, wg_ref, wl_ref, bg_ref, bl_ref, o_ref, wgb_ref, wlb_ref):
    i = pl.program_id(1)

    @pl.when(first_ref[i] == 1)
    def _():
        wgb_ref[...] = wg_ref[...].astype(bf16)
        wlb_ref[...] = wl_ref[...].astype(bf16)

    @pl.when(i < nv_ref[0])
    def _():
        x = xs_ref[...]
        gate = jnp.dot(x, wgb_ref[...], preferred_element_type=f32) + bg_ref[...]
        lin = jnp.dot(x, wlb_ref[...], preferred_element_type=f32) + bl_ref[...]
        gate = jnp.minimum(gate, SWIGLU_LIMIT)
        lin = jnp.clip(lin, -SWIGLU_LIMIT, SWIGLU_LIMIT)
        glu = gate * _sigmoid(SWIGLU_ALPHA * gate)
        o_ref[...] = ((lin + 1.0) * glu).astype(bf16)

    @pl.when(i >= nv_ref[0])
    def _():
        o_ref[...] = jnp.zeros_like(o_ref)


def _down_kernel(be_ref, first_ref, nv_ref, a_ref, w_ref, b_ref, rw_ref, o_ref, wb_ref):
    i = pl.program_id(1)

    @pl.when(first_ref[i] == 1)
    def _():
        wb_ref[...] = w_ref[...].astype(bf16)

    @pl.when(i < nv_ref[0])
    def _():
        y = jnp.dot(a_ref[...], wb_ref[...], preferred_element_type=f32) + b_ref[...]
        o_ref[...] = y * rw_ref[...]

    @pl.when(i >= nv_ref[0])
    def _():
        o_ref[...] = jnp.zeros_like(o_ref)


def _experts(xs, row_w, block_e, first, nvalid, w_up, b_up, w_down, b_down, layer):
    n_rows = xs.shape[0]
    nb = n_rows // MOE_TM
    nf = D_FF // UP_TF
    act = pl.pallas_call(
        _up_kernel,
        grid_spec=pltpu.PrefetchScalarGridSpec(
            num_scalar_prefetch=3,
            grid=(nf, nb),
            in_specs=[
                pl.BlockSpec((MOE_TM, D_MODEL), lambda j, i, be, fi, nv: (i, 0)),
                pl.BlockSpec((None, None, D_MODEL, UP_TF), lambda j, i, be, fi, nv: (layer, be[i], 0, j)),
                pl.BlockSpec((None, None, D_MODEL, UP_TF), lambda j, i, be, fi, nv: (layer, be[i], 0, nf + j)),
                pl.BlockSpec((None, None, 1, UP_TF), lambda j, i, be, fi, nv: (layer, be[i], 0, j)),
                pl.BlockSpec((None, None, 1, UP_TF), lambda j, i, be, fi, nv: (layer, be[i], 0, nf + j)),
            ],
            out_specs=pl.BlockSpec((MOE_TM, UP_TF), lambda j, i, be, fi, nv: (i, j)),
            scratch_shapes=[pltpu.VMEM((D_MODEL, UP_TF), bf16), pltpu.VMEM((D_MODEL, UP_TF), bf16)],
        ),
        out_shape=jax.ShapeDtypeStruct((n_rows, D_FF), bf16),
        compiler_params=_params(("arbitrary", "arbitrary")),
        name="moe_up",
    )(block_e, first, nvalid, xs, w_up, w_up, b_up, b_up)
    nd = D_MODEL // DOWN_TN
    return pl.pallas_call(
        _down_kernel,
        grid_spec=pltpu.PrefetchScalarGridSpec(
            num_scalar_prefetch=3,
            grid=(nd, nb),
            in_specs=[
                pl.BlockSpec((MOE_TM, D_FF), lambda j, i, be, fi, nv: (i, 0)),
                pl.BlockSpec((None, None, D_FF, DOWN_TN), lambda j, i, be, fi, nv: (layer, be[i], 0, j)),
                pl.BlockSpec((None, None, 1, DOWN_TN), lambda j, i, be, fi, nv: (layer, be[i], 0, j)),
                pl.BlockSpec((MOE_TM, 1), lambda j, i, be, fi, nv: (i, 0)),
            ],
            out_specs=pl.BlockSpec((MOE_TM, DOWN_TN), lambda j, i, be, fi, nv: (i, j)),
            scratch_shapes=[pltpu.VMEM((D_FF, DOWN_TN), bf16)],
        ),
        out_shape=jax.ShapeDtypeStruct((n_rows, D_MODEL), f32),
        compiler_params=_params(("arbitrary", "arbitrary")),
        name="moe_down",
    )(block_e, first, nvalid, act, w_down, b_down, row_w)


def _route(logits):
    t = logits.shape[0]
    tk = t * TOP_K
    top_logit, top_e = lax.top_k(logits, TOP_K)
    weights = jax.nn.softmax(top_logit, axis=-1)
    slot_e = top_e.reshape(tk)
    slot_tok = jnp.arange(tk, dtype=jnp.int32) // TOP_K
    order = jnp.argsort(slot_e)
    e_sorted = slot_e[order]
    counts = jnp.bincount(slot_e, length=N_EXPERTS)
    padded = (counts + MOE_TM - 1) // MOE_TM * MOE_TM
    pad_end = jnp.cumsum(padded)
    pad_start = pad_end - padded
    start = jnp.cumsum(counts) - counts
    dest = (pad_start[e_sorted] + jnp.arange(tk, dtype=jnp.int32) - start[e_sorted]).astype(jnp.int32)
    n_blocks = tk // MOE_TM + N_EXPERTS
    n_rows = n_blocks * MOE_TM
    row_tok = jnp.zeros((n_rows,), jnp.int32).at[dest].set(slot_tok[order])
    row_w = jnp.zeros((n_rows,), f32).at[dest].set(weights.reshape(tk)[order])
    blk_start = jnp.arange(n_blocks, dtype=jnp.int32) * MOE_TM
    block_e = jnp.minimum(jnp.searchsorted(pad_end, blk_start, side='right'), N_EXPERTS - 1).astype(jnp.int32)
    nvalid = (pad_end[-1] // MOE_TM).astype(jnp.int32).reshape(1)
    first = ((blk_start == pad_start[block_e]) & (blk_start < pad_end[-1])).astype(jnp.int32)
    pos = jnp.zeros((tk,), jnp.int32).at[order].set(dest).reshape(t, TOP_K)
    return row_tok, row_w.reshape(n_rows, 1), block_e, first, nvalid, pos


def kernel(x_prompt, x_sample, c_prompt, c_sample, state_hgrn, w_mod, b_mod, norm1_g, w_in, gmlp_ln_g, gmlp_ln_b, gmlp_w_s, gmlp_b_s, hgrn_lb_raw, hgrn_norm_g, w_branch_a, w_branch_b, w_out, norm2_g, router_w, router_b, exp_w_up, exp_b_up, exp_w_down, exp_b_down, final_norm_g):
    n_p, t_p, d = x_prompt.shape
    n_s, t_s, _ = x_sample.shape
    rows_p, rows_s = n_p * t_p, n_s * t_s

    n_c = n_p + n_s
    n_c_pad = -(-n_c // 8) * 8
    c_all = jnp.concatenate([c_prompt, c_sample, jnp.zeros((n_c_pad - n_c, d), f32)], axis=0)
    mod_all = _modulation(c_all, w_mod, b_mod)

    p = jax.nn.softmax(hgrn_lb_raw.astype(f32), axis=0)
    lower = jnp.maximum(jnp.cumsum(p, axis=0) - p[0:1], 0.0)

    s0_prompt = jnp.zeros((1, n_p, B_HEADS, B_DK, B_DK), f32)
    b_up4 = exp_b_up.reshape(DEPTH, N_EXPERTS, 1, 2 * D_FF)
    b_down4 = exp_b_down.reshape(DEPTH, N_EXPERTS, 1, D_MODEL)
    reps = CHUNK // t_s

    x_p, x_s = x_prompt, x_sample
    moe_out = None
    gate2_p = gate2_s = None
    states_p, states_s, v_rows = [], [], []
    for l in range(DEPTH):
        mod_p = mod_all[l, :n_p].reshape(n_p, 1, N_MOD * d)
        mod_s = mod_all[l, n_p:n_c].reshape(n_s, 1, N_MOD * d)
        if l == 0:
            _, h_p = _norm(x_p, norm1_g[l], mod=mod_p, shift_col=0, scale_col=1)
            _, h_s = _norm(x_s, norm1_g[l], mod=mod_s, shift_col=0, scale_col=1)
        else:
            x_p, h_p = _norm(x_p, norm1_g[l], y=moe_out, y_row0=0, gate=gate2_p, gate_col=5,
                             mod=mod_p, shift_col=0, scale_col=1, write_x=True)
            x_s, h_s = _norm(x_s, norm1_g[l], y=moe_out, y_row0=rows_p, gate=gate2_s, gate_col=5,
                             mod=mod_s, shift_col=0, scale_col=1, write_x=True)
        z = _in_proj(h_p, h_s, w_in, l)

        ws = gmlp_w_s[l]
        w_mix = jnp.stack([ws, jnp.tile(ws[:, :t_s, :t_s], (1, reps, reps))])
        bs = gmlp_b_s[l]
        b_mix = jnp.stack([jnp.repeat(bs.T, A_GW, axis=1),
                           jnp.repeat(jnp.tile(bs[:, :t_s].T, (reps, 1)), A_GW, axis=1)])
        ya, vn = _gmlp(z, w_mix, b_mix, gmlp_ln_g[l].reshape(1, D_A), gmlp_ln_b[l].reshape(1, D_A), rows_p, t_s)
        v_rows.append(vn[rows_p:].reshape(n_s, t_s, D_A))

        lb = lower[l].reshape(1, D_B)
        vecs = (jnp.log(lb), jnp.log1p(-lb), 1.0 - lb, hgrn_norm_g[l].reshape(1, D_B).astype(f32))
        yb_p, sp = _hgrn(z, s0_prompt, 0, vecs, n_p, t_p, 0, 1, HGRN_TB, HGRN_C)
        yb_s, ss = _hgrn(z, state_hgrn, l, vecs, n_s, t_s, rows_p, HGRN_NS, t_s, t_s)
        states_p.append(sp)
        states_s.append(ss)

        wa, wb, wo = w_branch_a[l].astype(bf16), w_branch_b[l].astype(bf16), w_out[l].astype(bf16)
        x_p, h2_p, lg_p = _merge(ya, yb_p, z, x_p, mod_p, wa, wb, wo, norm2_g[l], router_w[l], router_b[l], 0)
        x_s, h2_s, lg_s = _merge(ya, yb_s, z, x_s, mod_s, wa, wb, wo, norm2_g[l], router_w[l], router_b[l], rows_p)

        row_tok, row_w, block_e, first, nvalid, pos = _route(jnp.concatenate([lg_p, lg_s], axis=0))
        xs = jnp.concatenate([h2_p, h2_s], axis=0)[row_tok]
        y_rows = _experts(xs, row_w, block_e, first, nvalid, exp_w_up, b_up4, exp_w_down, b_down4, l)
        moe_out = jnp.sum(y_rows[pos], axis=1)
        gate2_p, gate2_s = mod_p, mod_s

    _, y_p = _norm(x_p, final_norm_g, y=moe_out, y_row0=0, gate=gate2_p, gate_col=5, out_dtype=f32)
    _, y_s = _norm(x_s, final_norm_g, y=moe_out, y_row0=rows_p, gate=gate2_s, gate_col=5, out_dtype=f32)
    return (y_p.reshape(n_p, t_p, d), y_s.reshape(n_s, t_s, d),
            jnp.stack(states_p), jnp.stack(states_s), jnp.stack(v_rows))
```

```python
import functools

import jax
import jax.numpy as jnp
from jax import lax
from jax.experimental import pallas as pl
from jax.experimental.pallas import tpu as pltpu

f32 = jnp.float32
bf16 = jnp.bfloat16

D_MODEL = 2048
DEPTH = 4
D_A = D_MODEL // 2
A_GROUPS = 8
A_GW = D_A // A_GROUPS
CHUNK = 128
D_B = D_MODEL // 2
B_DK = 128
B_HEADS = D_B // B_DK
IN_W = 2 * D_A + 4 * D_B + 2 * D_MODEL
N_EXPERTS = 32
TOP_K = 4
D_FF = D_MODEL
SWIGLU_LIMIT = 7.0
SWIGLU_ALPHA = 1.702
N_MOD = 6
EPS = 1e-6

VMEM_LIMIT_BYTES = 56 * 1024 * 1024

ROW_BLOCK = 256
IN_TM = 1024
IN_TN = 1024
MOE_TM = 256
UP_TF = 1024
DOWN_TN = 2048
HGRN_TB = 256
HGRN_C = 16
HGRN_NS = 8


def _params(sem):
    return pltpu.CompilerParams(dimension_semantics=sem, vmem_limit_bytes=VMEM_LIMIT_BYTES)


def _sigmoid(x):
    return jax.nn.sigmoid(x)


def _gelu(x):
    return 0.5 * x * (1.0 + lax.erf(x * 0.7071067811865476))


def _mod_kernel(c_ref, w_ref, b_ref, o_ref):
    c = c_ref[...]
    a = (c * _sigmoid(c)).astype(bf16)
    o_ref[...] = jnp.dot(a, w_ref[...].astype(bf16), preferred_element_type=f32) + b_ref[...]


def _modulation(c_all, w_mod, b_mod):
    n = c_all.shape[0]
    tn = 1024
    return pl.pallas_call(
        _mod_kernel,
        grid=(DEPTH, N_MOD * D_MODEL // tn),
        in_specs=[
            pl.BlockSpec((n, D_MODEL), lambda l, j: (0, 0)),
            pl.BlockSpec((None, D_MODEL, tn), lambda l, j: (l, 0, j)),
            pl.BlockSpec((None, 1, tn), lambda l, j: (l, 0, j)),
        ],
        out_specs=pl.BlockSpec((None, n, tn), lambda l, j: (l, 0, j)),
        out_shape=jax.ShapeDtypeStruct((DEPTH, n, N_MOD * D_MODEL), f32),
        compiler_params=_params(("arbitrary", "arbitrary")),
        name="modulation",
    )(c_all, w_mod, b_mod.reshape(DEPTH, 1, N_MOD * D_MODEL))


def _norm_kernel(has_res, has_mod, write_x, *refs):
    refs = list(refs)
    x_ref = refs.pop(0)
    if has_res:
        y_ref, gate_ref = refs.pop(0), refs.pop(0)
    g_ref = refs.pop(0)
    if has_mod:
        shift_ref, scale_ref = refs.pop(0), refs.pop(0)
    if write_x:
        xo_ref = refs.pop(0)
    h_ref = refs.pop(0)

    x = x_ref[...]
    nb, tb, d = x.shape
    if has_res:
        y = y_ref[0]
        for k in range(1, y_ref.shape[0]):
            y = y + y_ref[k]
        x = x + gate_ref[...] * y.reshape(nb, tb, d)
    if write_x:
        xo_ref[...] = x
    hn = x * lax.rsqrt(jnp.mean(x * x, axis=-1, keepdims=True) + EPS) * g_ref[...]
    if has_mod:
        hn = hn * (1.0 + scale_ref[...]) + shift_ref[...]
    h_ref[...] = hn.reshape(nb * tb, d).astype(h_ref.dtype)


def _norm(x3, g, *, y=None, y_row0=0, gate=None, gate_col=0, mod=None, shift_col=0, scale_col=0,
          write_x=False, out_dtype=bf16):
    n, t, d = x3.shape
    tb = min(t, ROW_BLOCK)
    nb = ROW_BLOCK // tb
    rows = nb * tb
    grid = (n // nb, t // tb)
    tpb = t // tb
    x_spec = pl.BlockSpec((nb, tb, d), lambda i, j: (i, j, 0))

    def vec_spec(col):
        return pl.BlockSpec((nb, 1, d), lambda i, j: (i, 0, col))

    ins, specs = [x3], [x_spec]
    if y is not None:
        base = y_row0 // rows
        ins += [y, gate]
        specs += [pl.BlockSpec((y.shape[0], rows, d), lambda i, j: (0, base + i * tpb + j, 0)), vec_spec(gate_col)]
    ins.append(g.reshape(1, 1, d))
    specs.append(pl.BlockSpec((1, 1, d), lambda i, j: (0, 0, 0)))
    if mod is not None:
        ins += [mod, mod]
        specs += [vec_spec(shift_col), vec_spec(scale_col)]
    out_shapes, out_specs = [], []
    if write_x:
        out_shapes.append(jax.ShapeDtypeStruct((n, t, d), f32))
        out_specs.append(x_spec)
    out_shapes.append(jax.ShapeDtypeStruct((n * t, d), out_dtype))
    out_specs.append(pl.BlockSpec((rows, d), lambda i, j: (i * tpb + j, 0)))
    res = pl.pallas_call(
        functools.partial(_norm_kernel, y is not None, mod is not None, write_x),
        grid=grid, in_specs=specs, out_specs=out_specs, out_shape=out_shapes,
        compiler_params=_params(("arbitrary", "arbitrary")),
        name="norm",
    )(*ins)
    return res if write_x else (None, res[0])


def _in_kernel(n_p, hp_ref, hs_ref, w_ref, o_ref, wb_ref):
    m = pl.program_id(1)

    @pl.when(m == 0)
    def _():
        wb_ref[...] = w_ref[...].astype(bf16)

    @pl.when(m < n_p)
    def _():
        o_ref[...] = jnp.dot(hp_ref[...], wb_ref[...], preferred_element_type=f32).astype(bf16)

    @pl.when(m >= n_p)
    def _():
        o_ref[...] = jnp.dot(hs_ref[...], wb_ref[...], preferred_element_type=f32).astype(bf16)


def _in_proj(h_p, h_s, w_in, layer):
    rows_p, rows_s = h_p.shape[0], h_s.shape[0]
    n_p, n_s = rows_p // IN_TM, rows_s // IN_TM
    return pl.pallas_call(
        functools.partial(_in_kernel, n_p),
        grid=(IN_W // IN_TN, n_p + n_s),
        in_specs=[
            pl.BlockSpec((IN_TM, D_MODEL), lambda j, m: (jnp.minimum(m, n_p - 1), 0)),
            pl.BlockSpec((IN_TM, D_MODEL), lambda j, m: (jnp.maximum(m - n_p, 0), 0)),
            pl.BlockSpec((None, D_MODEL, IN_TN), lambda j, m: (layer, 0, j)),
        ],
        out_specs=pl.BlockSpec((IN_TM, IN_TN), lambda j, m: (m, j)),
        out_shape=jax.ShapeDtypeStruct((rows_p + rows_s, IN_W), bf16),
        scratch_shapes=[pltpu.VMEM((D_MODEL, IN_TN), bf16)],
        compiler_params=_params(("arbitrary", "arbitrary")),
        name="in_proj",
    )(h_p, h_s, w_in)


def _gmlp_kernel(n_prompt_chunks, sample_shift, u_ref, v_ref, w_ref, b_ref, lg_ref, lb_ref, ya_ref, vn_ref):
    c = pl.program_id(0)
    u = _gelu(u_ref[...].astype(f32))
    v = _gelu(v_ref[...].astype(f32))
    vc = v - jnp.mean(v, axis=-1, keepdims=True)
    vn = vc * lax.rsqrt(jnp.mean(vc * vc, axis=-1, keepdims=True) + EPS) * lg_ref[...] + lb_ref[...]
    vn_ref[...] = vn
    shift = jnp.where(c >= n_prompt_chunks, sample_shift, 7)
    row = lax.broadcasted_iota(jnp.int32, (CHUNK, CHUNK), 0)
    col = lax.broadcasted_iota(jnp.int32, (CHUNK, CHUNK), 1)
    mask = ((row >> shift) == (col >> shift)) & (col <= row)
    vnb = vn.astype(bf16)
    mixed = []
    for g in range(A_GROUPS):
        wg = jnp.where(mask, w_ref[g], 0.0).astype(bf16)
        mixed.append(jnp.dot(wg, vnb[:, g * A_GW:(g + 1) * A_GW], preferred_element_type=f32))
    mixed = jnp.concatenate(mixed, axis=1) + b_ref[...]
    ya_ref[...] = (u * mixed).astype(bf16)


def _gmlp(z, w_mix, b_mix, ln_g, ln_b, rows_p, t_s):
    rows = z.shape[0]
    ncp = rows_p // CHUNK
    return pl.pallas_call(
        functools.partial(_gmlp_kernel, ncp, t_s.bit_length() - 1),
        grid=(rows // CHUNK,),
        in_specs=[
            pl.BlockSpec((CHUNK, D_A), lambda c: (c, 0)),
            pl.BlockSpec((CHUNK, D_A), lambda c: (c, 1)),
            pl.BlockSpec((None, A_GROUPS, CHUNK, CHUNK), lambda c: (jnp.where(c >= ncp, 1, 0), 0, 0, 0)),
            pl.BlockSpec((None, CHUNK, D_A), lambda c: (jnp.where(c >= ncp, 1, 0), 0, 0)),
            pl.BlockSpec((1, D_A), lambda c: (0, 0)),
            pl.BlockSpec((1, D_A), lambda c: (0, 0)),
        ],
        out_specs=[pl.BlockSpec((CHUNK, D_A), lambda c: (c, 0)), pl.BlockSpec((CHUNK, D_A), lambda c: (c, 0))],
        out_shape=[jax.ShapeDtypeStruct((rows, D_A), bf16), jax.ShapeDtypeStruct((rows, D_A), f32)],
        compiler_params=_params(("arbitrary",)),
        name="gmlp",
    )(z, z, w_mix, b_mix, ln_g, ln_b)


def _hgrn_kernel(ns, tb, c, q_ref, f_ref, i_ref, g_ref, s0_ref, llb_ref, l1m_ref, oml_ref, hng_ref, *rest):
    yb_ref, s_ref = rest[-2:]

    @pl.when(pl.program_id(1) == 0)
    def _():
        s_ref[...] = s0_ref[...]

    row = lax.broadcasted_iota(jnp.int32, (c, c), 0)
    col = lax.broadcasted_iota(jnp.int32, (c, c), 1)
    tri = (col <= row).astype(f32)
    rows1 = lax.broadcasted_iota(jnp.int32, (c, 1), 0)
    ones = jnp.ones((B_DK, B_DK), bf16)
    llb, l1m, oml, hng = llb_ref[...], l1m_ref[...], oml_ref[...], hng_ref[...]

    def chunk(si, r0):
        zq = q_ref[pl.ds(r0, c), :].astype(f32)
        zf = f_ref[pl.ds(r0, c), :].astype(f32)
        zi = i_ref[pl.ds(r0, c), :].astype(f32)
        zg = g_ref[pl.ds(r0, c), :].astype(f32)
        qs = zq * _sigmoid(zq)
        a = l1m + jnp.minimum(zf, 0.0) - jnp.log1p(jnp.exp(-jnp.abs(zf)))
        logf = jnp.maximum(a, llb) + jnp.log1p(jnp.exp(-jnp.abs(a - llb)))
        kk = oml * _sigmoid(-zf)
        gc = jnp.dot(tri, logf, preferred_element_type=f32, precision=lax.Precision.HIGHEST)
        g_last = gc[c - 1:c, :]
        q_dec = (qs * jnp.exp(gc)).astype(bf16)
        k_dec = (kk * jnp.exp(g_last - gc)).astype(bf16)
        vb = zi.astype(bf16)
        dec = jnp.exp(jnp.concatenate([g_last[:, h * B_DK:(h + 1) * B_DK] for h in range(B_HEADS)], axis=0))
        dec_t = dec.T
        gate = zg * _sigmoid(zg)
        for h in range(B_HEADS):
            sl = slice(h * B_DK, (h + 1) * B_DK)
            s_prev = s_ref[si, h]
            o = jnp.dot(q_dec[:, sl], s_prev.astype(bf16), preferred_element_type=f32)
            qh, gh, kh, vh = qs[:, sl], gc[:, sl], kk[:, sl], zi[:, sl]
            terms = []
            for s in range(c):
                e = jnp.where(rows1 >= s, jnp.exp(gh - gh[s:s + 1, :]), 0.0)
                terms.append(qh * (kh[s:s + 1, :] * e))
            a_rows = jnp.dot(jnp.concatenate(terms, axis=0).astype(bf16), ones, preferred_element_type=f32)
            for s in range(c):
                o = o + a_rows[s * c:(s + 1) * c, :] * vh[s:s + 1, :]
            upd = lax.dot_general(k_dec[:, sl], vb[:, sl], (((0,), (0,)), ((), ())),
                                  preferred_element_type=f32)
            s_ref[si, h] = dec_t[:, h:h + 1] * s_prev + upd
            on = o * lax.rsqrt(jnp.mean(o * o, axis=-1, keepdims=True) + EPS) * hng[:, sl]
            yb_ref[pl.ds(r0, c), sl] = (on * gate[:, sl]).astype(bf16)

    n_chunks = tb // c
    for si in range(ns):
        if n_chunks == 1:
            chunk(si, si * tb)
        else:
            def body(ci, carry, si=si):
                chunk(si, pl.multiple_of(si * tb + ci * c, c))
                return carry
            lax.fori_loop(0, n_chunks, body, 0)


def _hgrn(z, s0, s0_layer, vecs, n_seq, t, row0, ns, tb, c, layer, states):
    rows = ns * tb
    tpb = t // tb
    base = row0 // rows
    grid = (n_seq // ns, tpb)

    def zspec(colblk):
        return pl.BlockSpec((rows, D_B), lambda n, j: (base + n * tpb + j, colblk))

    vspec = pl.BlockSpec((1, D_B), lambda n, j: (0, 0))
    ins = [z, z, z, z, s0, *vecs]
    specs = [zspec(2), zspec(3), zspec(4), zspec(5),
             pl.BlockSpec((None, ns, B_HEADS, B_DK, B_DK), lambda n, j: (s0_layer, n, 0, 0, 0)),
             vspec, vspec, vspec, vspec]
    aliases = {}
    if states is not None:
        aliases = {len(ins): 1}
        ins.append(states)
        specs.append(pl.BlockSpec(memory_space=pl.ANY))
    yb, states = pl.pallas_call(
        functools.partial(_hgrn_kernel, ns, tb, c),
        grid=grid,
        in_specs=specs,
        out_specs=[pl.BlockSpec((rows, D_B), lambda n, j: (n * tpb + j, 0)),
                   pl.BlockSpec((None, ns, B_HEADS, B_DK, B_DK), lambda n, j: (layer, n, 0, 0, 0))],
        out_shape=[jax.ShapeDtypeStruct((n_seq * t, D_B), bf16),
                   jax.ShapeDtypeStruct((DEPTH, n_seq, B_HEADS, B_DK, B_DK), f32)],
        input_output_aliases=aliases,
        compiler_params=_params(("arbitrary", "arbitrary")),
        name="hgrn",
    )(*ins)
    return yb, states


def _merge_kernel(ya_ref, yb_ref, ga_ref, gb_ref, x_ref, mg_ref, wa_ref, wb_ref, wo_ref, n2_ref,
                  sh_ref, sc_ref, rw_ref, rb_ref, *rest):
    xo_ref, h2_ref, lg_ref = rest[-3:]
    a = jnp.dot(ya_ref[...], wa_ref[...], preferred_element_type=f32)
    b = jnp.dot(yb_ref[...], wb_ref[...], preferred_element_type=f32)
    merged = _sigmoid(ga_ref[...].astype(f32)) * a + _sigmoid(gb_ref[...].astype(f32)) * b
    mix = jnp.dot(merged.astype(bf16), wo_ref[...], preferred_element_type=f32)
    x = x_ref[...]
    nb, tb, d = x.shape
    x = x + mg_ref[...] * mix.reshape(nb, tb, d)
    xo_ref[...] = x
    hn = x * lax.rsqrt(jnp.mean(x * x, axis=-1, keepdims=True) + EPS) * n2_ref[...]
    hn = (hn * (1.0 + sc_ref[...]) + sh_ref[...]).reshape(nb * tb, d)
    h2_ref[...] = hn
    lg_ref[...] = jnp.dot(hn, rw_ref[...], preferred_element_type=f32,
                          precision=lax.Precision.HIGHEST) + rb_ref[...]


def _merge(ya, yb_g, z, x3, mod, wa, wb, wo, n2g, rw, rb, row0, shared):
    n, t, d = x3.shape
    total = z.shape[0]
    tb = min(t, ROW_BLOCK)
    nb = ROW_BLOCK // tb
    rows = nb * tb
    tpb = t // tb
    base = row0 // rows
    grid = (n // nb, tpb)
    ga_blk = (2 * D_A + 4 * D_B) // D_MODEL

    def rspec(width, colblk, b):
        return pl.BlockSpec((rows, width), lambda i, j: (b + i * tpb + j, colblk))

    def vec_spec(col):
        return pl.BlockSpec((nb, 1, d), lambda i, j: (i, 0, col))

    def const(shape):
        return pl.BlockSpec(shape, lambda i, j: tuple(0 for _ in shape), pipeline_mode=pl.Buffered(1))

    x_spec = pl.BlockSpec((nb, tb, d), lambda i, j: (i, j, 0))
    ins = [ya, yb_g, z, z, x3, mod, wa, wb, wo, n2g.reshape(1, 1, d), mod, mod, rw, rb.reshape(1, N_EXPERTS)]
    specs = [rspec(D_A, 0, base), rspec(D_B, 0, 0), rspec(d, ga_blk, base), rspec(d, ga_blk + 1, base),
             x_spec, vec_spec(2),
             const((D_A, d)), const((D_B, d)), const((d, d)), const((1, 1, d)),
             vec_spec(3), vec_spec(4), const((d, N_EXPERTS)), const((1, N_EXPERTS))]
    aliases = {}
    if shared is not None:
        aliases = {len(ins): 1, len(ins) + 1: 2}
        ins += list(shared)
        specs += [pl.BlockSpec(memory_space=pl.ANY), pl.BlockSpec(memory_space=pl.ANY)]
    x_new, h2, logits = pl.pallas_call(
        _merge_kernel,
        grid=grid,
        in_specs=specs,
        out_specs=[x_spec, rspec(d, 0, base), rspec(N_EXPERTS, 0, base)],
        out_shape=[jax.ShapeDtypeStruct((n, t, d), f32), jax.ShapeDtypeStruct((total, d), f32),
                   jax.ShapeDtypeStruct((total, N_EXPERTS), f32)],
        input_output_aliases=aliases,
        compiler_params=_params(("arbitrary", "arbitrary")),
        name="merge",
    )(*ins)
    return x_new, (h2, logits)


def _up_kernel(be_ref, first_ref, nv_ref, xs_ref, wg_ref, wl_ref, bg_ref, bl_ref, o_ref, wgb_ref, wlb_ref):
    i = pl.program_id(1)

    @pl.when(first_ref[i] == 1)
    def _():
        wgb_ref[...] = wg_ref[...].astype(bf16)
        wlb_ref[...] = wl_ref[...].astype(bf16)

    @pl.when(i < nv_ref[0])
    def _():
        x = xs_ref[...].astype(bf16)
        gate = jnp.dot(x, wgb_ref[...], preferred_element_type=f32) + bg_ref[...]
        lin = jnp.dot(x, wlb_ref[...], preferred_element_type=f32) + bl_ref[...]
        gate = jnp.minimum(gate, SWIGLU_LIMIT)
        lin = jnp.clip(lin, -SWIGLU_LIMIT, SWIGLU_LIMIT)
        glu = gate * _sigmoid(SWIGLU_ALPHA * gate)
        o_ref[...] = ((lin + 1.0) * glu).astype(bf16)

    @pl.when(i >= nv_ref[0])
    def _():
        o_ref[...] = jnp.zeros_like(o_ref)


def _down_kernel(be_ref, first_ref, nv_ref, a_ref, w_ref, b_ref, rw_ref, o_ref, wb_ref):
    i = pl.program_id(1)

    @pl.when(first_ref[i] == 1)
    def _():
        wb_ref[...] = w_ref[...].astype(bf16)

    @pl.when(i < nv_ref[0])
    def _():
        y = jnp.dot(a_ref[...], wb_ref[...], preferred_element_type=f32) + b_ref[...]
        o_ref[...] = y * rw_ref[...]

    @pl.when(i >= nv_ref[0])
    def _():
        o_ref[...] = jnp.zeros_like(o_ref)


def _experts(xs, row_w, block_e, first, nvalid, w_up, b_up, w_down, b_down, layer):
    n_rows = xs.shape[0]
    nb = n_rows // MOE_TM
    nf = D_FF // UP_TF
    act = pl.pallas_call(
        _up_kernel,
        grid_spec=pltpu.PrefetchScalarGridSpec(
            num_scalar_prefetch=3,
            grid=(nf, nb),
            in_specs=[
                pl.BlockSpec((MOE_TM, D_MODEL), lambda j, i, be, fi, nv: (i, 0)),
                pl.BlockSpec((None, None, D_MODEL, UP_TF), lambda j, i, be, fi, nv: (layer, be[i], 0, j)),
                pl.BlockSpec((None, None, D_MODEL, UP_TF), lambda j, i, be, fi, nv: (layer, be[i], 0, nf + j)),
                pl.BlockSpec((None, None, 1, UP_TF), lambda j, i, be, fi, nv: (layer, be[i], 0, j)),
                pl.BlockSpec((None, None, 1, UP_TF), lambda j, i, be, fi, nv: (layer, be[i], 0, nf + j)),
            ],
            out_specs=pl.BlockSpec((MOE_TM, UP_TF), lambda j, i, be, fi, nv: (i, j)),
            scratch_shapes=[pltpu.VMEM((D_MODEL, UP_TF), bf16), pltpu.VMEM((D_MODEL, UP_TF), bf16)],
        ),
        out_shape=jax.ShapeDtypeStruct((n_rows, D_FF), bf16),
        compiler_params=_params(("arbitrary", "arbitrary")),
        name="moe_up",
    )(block_e, first, nvalid, xs, w_up, w_up, b_up, b_up)
    nd = D_MODEL // DOWN_TN
    return pl.pallas_call(
        _down_kernel,
        grid_spec=pltpu.PrefetchScalarGridSpec(
            num_scalar_prefetch=3,
            grid=(nd, nb),
            in_specs=[
                pl.BlockSpec((MOE_TM, D_FF), lambda j, i, be, fi, nv: (i, 0)),
                pl.BlockSpec((None, None, D_FF, DOWN_TN), lambda j, i, be, fi, nv: (layer, be[i], 0, j)),
                pl.BlockSpec((None, None, 1, DOWN_TN), lambda j, i, be, fi, nv: (layer, be[i], 0, j)),
                pl.BlockSpec((MOE_TM, 1), lambda j, i, be, fi, nv: (i, 0)),
            ],
            out_specs=pl.BlockSpec((MOE_TM, DOWN_TN), lambda j, i, be, fi, nv: (i, j)),
            scratch_shapes=[pltpu.VMEM((D_FF, DOWN_TN), bf16)],
        ),
        out_shape=jax.ShapeDtypeStruct((n_rows, D_MODEL), f32),
        compiler_params=_params(("arbitrary", "arbitrary")),
        name="moe_down",
    )(block_e, first, nvalid, act, w_down, b_down, row_w)


def _route(logits):
    t = logits.shape[0]
    tk = t * TOP_K
    top_logit, top_e = lax.top_k(logits, TOP_K)
    weights = jax.nn.softmax(top_logit, axis=-1)
    experts = jnp.arange(N_EXPERTS, dtype=jnp.int32)
    onehot = top_e[:, :, None] == experts[None, None, :]
    member = jnp.sum(onehot, axis=1, dtype=jnp.int32)
    csum = jnp.cumsum(member, axis=0)
    counts = csum[-1]
    rank = csum - member
    padded = (counts + MOE_TM - 1) // MOE_TM * MOE_TM
    pad_end = jnp.cumsum(padded)
    pad_start = pad_end - padded
    start = jnp.cumsum(counts) - counts
    pos = jnp.sum(jnp.where(onehot, (pad_start[None, :] + rank)[:, None, :], 0), axis=-1).astype(jnp.int32)
    n_blocks = tk // MOE_TM + N_EXPERTS
    n_rows = n_blocks * MOE_TM
    blk_start = jnp.arange(n_blocks, dtype=jnp.int32) * MOE_TM
    block_e = jnp.minimum(jnp.sum(pad_end[None, :] <= blk_start[:, None], axis=1), N_EXPERTS - 1).astype(jnp.int32)
    nvalid = (pad_end[-1] // MOE_TM).astype(jnp.int32).reshape(1)
    first = ((blk_start == pad_start[block_e]) & (blk_start < pad_end[-1])).astype(jnp.int32)
    order = jnp.argsort(top_e.reshape(tk), stable=True).astype(jnp.int32)
    row = jnp.arange(n_rows, dtype=jnp.int32)
    row_e = jnp.repeat(block_e, MOE_TM)
    offset = row - pad_start[row_e]
    live = offset < counts[row_e]
    slot = order[jnp.clip(start[row_e] + offset, 0, tk - 1)]
    row_tok = jnp.where(live, slot // TOP_K, 0)
    row_w = jnp.where(live, weights.reshape(tk)[slot], 0.0)
    return row_tok, row_w.reshape(n_rows, 1), block_e, first, nvalid, pos


def kernel(x_prompt, x_sample, c_prompt, c_sample, state_hgrn, w_mod, b_mod, norm1_g, w_in, gmlp_ln_g, gmlp_ln_b, gmlp_w_s, gmlp_b_s, hgrn_lb_raw, hgrn_norm_g, w_branch_a, w_branch_b, w_out, norm2_g, router_w, router_b, exp_w_up, exp_b_up, exp_w_down, exp_b_down, final_norm_g):
    n_p, t_p, d = x_prompt.shape
    n_s, t_s, _ = x_sample.shape
    rows_p, rows_s = n_p * t_p, n_s * t_s

    n_c = n_p + n_s
    n_c_pad = -(-n_c // 8) * 8
    c_all = jnp.concatenate([c_prompt, c_sample, jnp.zeros((n_c_pad - n_c, d), f32)], axis=0)
    mod_all = _modulation(c_all, w_mod, b_mod)

    p = jax.nn.softmax(hgrn_lb_raw.astype(f32), axis=0)
    lower = jnp.maximum(jnp.cumsum(p, axis=0) - p[0:1], 0.0)

    s0_prompt = jnp.zeros((1, n_p, B_HEADS, B_DK, B_DK), f32)
    b_up4 = exp_b_up.reshape(DEPTH, N_EXPERTS, 1, 2 * D_FF)
    b_down4 = exp_b_down.reshape(DEPTH, N_EXPERTS, 1, D_MODEL)
    reps = CHUNK // t_s

    x_p, x_s = x_prompt, x_sample
    moe_out = None
    gate2_p = gate2_s = None
    states_p = states_s = None
    v_rows = []
    for l in range(DEPTH):
        mod_p = mod_all[l, :n_p].reshape(n_p, 1, N_MOD * d)
        mod_s = mod_all[l, n_p:n_c].reshape(n_s, 1, N_MOD * d)
        if l == 0:
            _, h_p = _norm(x_p, norm1_g[l], mod=mod_p, shift_col=0, scale_col=1)
            _, h_s = _norm(x_s, norm1_g[l], mod=mod_s, shift_col=0, scale_col=1)
        else:
            x_p, h_p = _norm(x_p, norm1_g[l], y=moe_out, y_row0=0, gate=gate2_p, gate_col=5,
                             mod=mod_p, shift_col=0, scale_col=1, write_x=True)
            x_s, h_s = _norm(x_s, norm1_g[l], y=moe_out, y_row0=rows_p, gate=gate2_s, gate_col=5,
                             mod=mod_s, shift_col=0, scale_col=1, write_x=True)
        z = _in_proj(h_p, h_s, w_in, l)

        ws = gmlp_w_s[l]
        w_mix = jnp.stack([ws, jnp.tile(ws[:, :t_s, :t_s], (1, reps, reps))])
        bs = gmlp_b_s[l]
        b_mix = jnp.stack([jnp.repeat(bs.T, A_GW, axis=1),
                           jnp.repeat(jnp.tile(bs[:, :t_s].T, (reps, 1)), A_GW, axis=1)])
        ya, vn = _gmlp(z, w_mix, b_mix, gmlp_ln_g[l].reshape(1, D_A), gmlp_ln_b[l].reshape(1, D_A), rows_p, t_s)
        v_rows.append(vn[rows_p:].reshape(n_s, t_s, D_A))

        lb = lower[l].reshape(1, D_B)
        vecs = (jnp.log(lb), jnp.log1p(-lb), 1.0 - lb, hgrn_norm_g[l].reshape(1, D_B).astype(f32))
        yb_p, states_p = _hgrn(z, s0_prompt, 0, vecs, n_p, t_p, 0, 1, HGRN_TB, HGRN_C, l, states_p)
        yb_s, states_s = _hgrn(z, state_hgrn, l, vecs, n_s, t_s, rows_p, HGRN_NS, t_s, t_s, l, states_s)

        wa, wb, wo = w_branch_a[l].astype(bf16), w_branch_b[l].astype(bf16), w_out[l].astype(bf16)
        x_p, shared = _merge(ya, yb_p, z, x_p, mod_p, wa, wb, wo, norm2_g[l], router_w[l], router_b[l], 0, None)
        x_s, (h2, logits) = _merge(ya, yb_s, z, x_s, mod_s, wa, wb, wo, norm2_g[l], router_w[l], router_b[l],
                                   rows_p, shared)

        row_tok, row_w, block_e, first, nvalid, pos = _route(logits)
        xs = h2[row_tok]
        y_rows = _experts(xs, row_w, block_e, first, nvalid, exp_w_up, b_up4, exp_w_down, b_down4, l)
        moe_out = y_rows[pos.T.reshape(-1)].reshape(TOP_K, rows_p + rows_s, d)
        gate2_p, gate2_s = mod_p, mod_s

    _, y_p = _norm(x_p, final_norm_g, y=moe_out, y_row0=0, gate=gate2_p, gate_col=5, out_dtype=f32)
    _, y_s = _norm(x_s, final_norm_g, y=moe_out, y_row0=rows_p, gate=gate2_s, gate_col=5, out_dtype=f32)
    return (y_p.reshape(n_p, t_p, d), y_s.reshape(n_s, t_s, d),
            states_p, states_s, jnp.stack(v_rows))
```

```python
import functools

import jax
import jax.numpy as jnp
from jax import lax
from jax.experimental import pallas as pl
from jax.experimental.pallas import tpu as pltpu

f32 = jnp.float32
bf16 = jnp.bfloat16

D_MODEL = 2048
DEPTH = 4
D_A = D_MODEL // 2
A_GROUPS = 8
A_GW = D_A // A_GROUPS
CHUNK = 128
D_B = D_MODEL // 2
B_DK = 128
B_HEADS = D_B // B_DK
IN_W = 2 * D_A + 4 * D_B + 2 * D_MODEL
N_EXPERTS = 32
TOP_K = 4
D_FF = D_MODEL
SWIGLU_LIMIT = 7.0
SWIGLU_ALPHA = 1.702
N_MOD = 6
EPS = 1e-6

VMEM_LIMIT_BYTES = 56 * 1024 * 1024

SEG = 8
ROW_BLOCK = 256
SEGS = ROW_BLOCK // SEG
IN_TM = 1024
IN_TN = 1024
MOE_TM = 256
UP_TF = 1024
HGRN_TB = 256
HGRN_C = 16
HGRN_NS = 8


def _params(sem):
    return pltpu.CompilerParams(dimension_semantics=sem, vmem_limit_bytes=VMEM_LIMIT_BYTES)


def _sigmoid(x):
    return jax.nn.sigmoid(x)


def _gelu(x):
    return 0.5 * x * (1.0 + lax.erf(x * 0.7071067811865476))


def _mod_kernel(c_ref, w_ref, b_ref, o_ref):
    c = c_ref[...]
    a = (c * _sigmoid(c)).astype(bf16)
    o_ref[...] = jnp.dot(a, w_ref[...].astype(bf16), preferred_element_type=f32) + b_ref[...]


def _modulation(c_all, w_mod, b_mod):
    n = c_all.shape[0]
    tn = 1024
    return pl.pallas_call(
        _mod_kernel,
        grid=(DEPTH, N_MOD * D_MODEL // tn),
        in_specs=[
            pl.BlockSpec((n, D_MODEL), lambda l, j: (0, 0)),
            pl.BlockSpec((None, D_MODEL, tn), lambda l, j: (l, 0, j)),
            pl.BlockSpec((None, 1, tn), lambda l, j: (l, 0, j)),
        ],
        out_specs=pl.BlockSpec((None, n, tn), lambda l, j: (l, 0, j)),
        out_shape=jax.ShapeDtypeStruct((DEPTH, n, N_MOD * D_MODEL), f32),
        compiler_params=_params(("arbitrary", "arbitrary")),
        name="modulation",
    )(c_all, w_mod, b_mod.reshape(DEPTH, 1, N_MOD * D_MODEL))


class _Rows:
    def __init__(self, n_p, t_p, n_s, t_s):
        assert t_s == SEG and t_p % ROW_BLOCK == 0 and (n_s * t_s) % ROW_BLOCK == 0
        self.n_p, self.n_s = n_p, n_s
        self.blocks_per_seq = t_p // ROW_BLOCK
        self.prompt_blocks = n_p * self.blocks_per_seq
        self.n_blocks = self.prompt_blocks + n_s * t_s // ROW_BLOCK
        self.rows = self.n_blocks * ROW_BLOCK

    def x_spec(self):
        return pl.BlockSpec((SEGS, SEG, D_MODEL), lambda i: (i, 0, 0))

    def row_spec(self, width, colblk=0):
        return pl.BlockSpec((ROW_BLOCK, width), lambda i: (i, colblk))

    def mod_specs(self, layer, col):
        pb, bps, n_p, n_s = self.prompt_blocks, self.blocks_per_seq, self.n_p, self.n_s
        return [pl.BlockSpec((None, 1, 1, D_MODEL), lambda i: (layer, n_s + jnp.minimum(i // bps, n_p - 1), 0, col)),
                pl.BlockSpec((None, SEGS, 1, D_MODEL), lambda i: (layer, jnp.maximum(i - pb, 0), 0, col))]


def _pick_mod(prompt_blocks, p_ref, s_ref):
    return jnp.where(pl.program_id(0) < prompt_blocks, p_ref[...], s_ref[...])


def _norm_kernel(prompt_blocks, has_res, has_mod, *refs):
    refs = list(refs)
    x_ref = refs.pop(0)
    if has_res:
        y_ref, gp_ref, gs_ref = refs.pop(0), refs.pop(0), refs.pop(0)
    g_ref = refs.pop(0)
    if has_mod:
        shp_ref, shs_ref, scp_ref, scs_ref = (refs.pop(0) for _ in range(4))
    if has_res:
        xo_ref = refs.pop(0)
    h_ref = refs.pop(0)

    x = x_ref[...]
    nb, tb, d = x.shape
    if has_res:
        y = y_ref[0]
        for k in range(1, y_ref.shape[0]):
            y = y + y_ref[k]
        x = x + _pick_mod(prompt_blocks, gp_ref, gs_ref) * y.reshape(nb, tb, d)
        xo_ref[...] = x
    hn = x * lax.rsqrt(jnp.mean(x * x, axis=-1, keepdims=True) + EPS) * g_ref[...]
    if has_mod:
        hn = hn * (1.0 + _pick_mod(prompt_blocks, scp_ref, scs_ref)) + _pick_mod(prompt_blocks, shp_ref, shs_ref)
    h_ref[...] = hn.reshape(nb * tb, d).astype(h_ref.dtype)


def _norm(geo, x3, g, *, y=None, mod=None, res_layer=None, mod_layer=None, out_dtype=bf16):
    d = D_MODEL
    has_res, has_mod = y is not None, mod_layer is not None
    ins, specs = [x3], [geo.x_spec()]
    if has_res:
        ins += [y, mod, mod]
        specs += [pl.BlockSpec((y.shape[0], ROW_BLOCK, d), lambda i: (0, i, 0))] + geo.mod_specs(res_layer, 5)
    ins.append(g.reshape(1, 1, d))
    specs.append(pl.BlockSpec((1, 1, d), lambda i: (0, 0, 0)))
    if has_mod:
        ins += [mod] * 4
        specs += geo.mod_specs(mod_layer, 0) + geo.mod_specs(mod_layer, 1)
    out_shapes, out_specs = [], []
    if has_res:
        out_shapes.append(jax.ShapeDtypeStruct(x3.shape, f32))
        out_specs.append(geo.x_spec())
    out_shapes.append(jax.ShapeDtypeStruct((geo.rows, d), out_dtype))
    out_specs.append(geo.row_spec(d))
    res = pl.pallas_call(
        functools.partial(_norm_kernel, geo.prompt_blocks, has_res, has_mod),
        grid=(geo.n_blocks,), in_specs=specs, out_specs=out_specs, out_shape=out_shapes,
        compiler_params=_params(("arbitrary",)),
        name="norm",
    )(*ins)
    return res if has_res else (None, res[0])


def _in_kernel(h_ref, w_ref, o_ref, wb_ref):
    @pl.when(pl.program_id(1) == 0)
    def _():
        wb_ref[...] = w_ref[...].astype(bf16)

    o_ref[...] = jnp.dot(h_ref[...], wb_ref[...], preferred_element_type=f32).astype(bf16)


def _in_proj(h, w_in, layer):
    rows = h.shape[0]
    return pl.pallas_call(
        _in_kernel,
        grid=(IN_W // IN_TN, rows // IN_TM),
        in_specs=[
            pl.BlockSpec((IN_TM, D_MODEL), lambda j, m: (m, 0)),
            pl.BlockSpec((None, D_MODEL, IN_TN), lambda j, m: (layer, 0, j)),
        ],
        out_specs=pl.BlockSpec((IN_TM, IN_TN), lambda j, m: (m, j)),
        out_shape=jax.ShapeDtypeStruct((rows, IN_W), bf16),
        scratch_shapes=[pltpu.VMEM((D_MODEL, IN_TN), bf16)],
        compiler_params=_params(("arbitrary", "arbitrary")),
        name="in_proj",
    )(h, w_in)


def _gmlp_kernel(n_prompt_chunks, sample_shift, u_ref, v_ref, w_ref, b_ref, lg_ref, lb_ref, ya_ref, vn_ref):
    c = pl.program_id(0)
    u = _gelu(u_ref[...].astype(f32))
    v = _gelu(v_ref[...].astype(f32))
    vc = v - jnp.mean(v, axis=-1, keepdims=True)
    vn = vc * lax.rsqrt(jnp.mean(vc * vc, axis=-1, keepdims=True) + EPS) * lg_ref[...] + lb_ref[...]
    vn_ref[...] = vn
    shift = jnp.where(c >= n_prompt_chunks, sample_shift, 7)
    row = lax.broadcasted_iota(jnp.int32, (CHUNK, CHUNK), 0)
    col = lax.broadcasted_iota(jnp.int32, (CHUNK, CHUNK), 1)
    mask = ((row >> shift) == (col >> shift)) & (col <= row)
    vnb = vn.astype(bf16)
    mixed = []
    for g in range(A_GROUPS):
        wg = jnp.where(mask, w_ref[g], 0.0).astype(bf16)
        mixed.append(jnp.dot(wg, vnb[:, g * A_GW:(g + 1) * A_GW], preferred_element_type=f32))
    mixed = jnp.concatenate(mixed, axis=1) + b_ref[...]
    ya_ref[...] = (u * mixed).astype(bf16)


def _gmlp(z, w_mix, b_mix, ln_g, ln_b, rows_p, t_s):
    rows = z.shape[0]
    ncp = rows_p // CHUNK
    return pl.pallas_call(
        functools.partial(_gmlp_kernel, ncp, t_s.bit_length() - 1),
        grid=(rows // CHUNK,),
        in_specs=[
            pl.BlockSpec((CHUNK, D_A), lambda c: (c, 0)),
            pl.BlockSpec((CHUNK, D_A), lambda c: (c, 1)),
            pl.BlockSpec((None, A_GROUPS, CHUNK, CHUNK), lambda c: (jnp.where(c >= ncp, 1, 0), 0, 0, 0)),
            pl.BlockSpec((None, CHUNK, D_A), lambda c: (jnp.where(c >= ncp, 1, 0), 0, 0)),
            pl.BlockSpec((1, D_A), lambda c: (0, 0)),
            pl.BlockSpec((1, D_A), lambda c: (0, 0)),
        ],
        out_specs=[pl.BlockSpec((CHUNK, D_A), lambda c: (c, 0)), pl.BlockSpec((CHUNK, D_A), lambda c: (c, 0))],
        out_shape=[jax.ShapeDtypeStruct((rows, D_A), bf16), jax.ShapeDtypeStruct((rows, D_A), f32)],
        compiler_params=_params(("arbitrary",)),
        name="gmlp",
    )(z, z, w_mix, b_mix, ln_g, ln_b)


def _hgrn_kernel(ns, tb, c, q_ref, f_ref, i_ref, g_ref, s0_ref, llb_ref, l1m_ref, oml_ref, hng_ref, all_ref,
                 yb_ref, s_ref):
    del all_ref

    @pl.when(pl.program_id(1) == 0)
    def _():
        s_ref[...] = s0_ref[...]

    row = lax.broadcasted_iota(jnp.int32, (c, c), 0)
    col = lax.broadcasted_iota(jnp.int32, (c, c), 1)
    tri = (col <= row).astype(f32)
    rows1 = lax.broadcasted_iota(jnp.int32, (c, 1), 0)
    ones = jnp.ones((B_DK, B_DK), bf16)
    llb, l1m, oml, hng = llb_ref[...], l1m_ref[...], oml_ref[...], hng_ref[...]

    def chunk(si, r0):
        zq = q_ref[pl.ds(r0, c), :].astype(f32)
        zf = f_ref[pl.ds(r0, c), :].astype(f32)
        zi = i_ref[pl.ds(r0, c), :].astype(f32)
        zg = g_ref[pl.ds(r0, c), :].astype(f32)
        qs = zq * _sigmoid(zq)
        a = l1m + jnp.minimum(zf, 0.0) - jnp.log1p(jnp.exp(-jnp.abs(zf)))
        logf = jnp.maximum(a, llb) + jnp.log1p(jnp.exp(-jnp.abs(a - llb)))
        kk = oml * _sigmoid(-zf)
        gc = jnp.dot(tri, logf, preferred_element_type=f32, precision=lax.Precision.HIGHEST)
        g_last = gc[c - 1:c, :]
        q_dec = (qs * jnp.exp(gc)).astype(bf16)
        k_dec = (kk * jnp.exp(g_last - gc)).astype(bf16)
        vb = zi.astype(bf16)
        dec = jnp.exp(jnp.concatenate([g_last[:, h * B_DK:(h + 1) * B_DK] for h in range(B_HEADS)], axis=0))
        dec_t = dec.T
        gate = zg * _sigmoid(zg)
        for h in range(B_HEADS):
            sl = slice(h * B_DK, (h + 1) * B_DK)
            s_prev = s_ref[si, h]
            o = jnp.dot(q_dec[:, sl], s_prev.astype(bf16), preferred_element_type=f32)
            qh, gh, kh, vh = qs[:, sl], gc[:, sl], kk[:, sl], zi[:, sl]
            terms = []
            for s in range(c):
                e = jnp.where(rows1 >= s, jnp.exp(gh - gh[s:s + 1, :]), 0.0)
                terms.append(qh * (kh[s:s + 1, :] * e))
            a_rows = jnp.dot(jnp.concatenate(terms, axis=0).astype(bf16), ones, preferred_element_type=f32)
            for s in range(c):
                o = o + a_rows[s * c:(s + 1) * c, :] * vh[s:s + 1, :]
            upd = lax.dot_general(k_dec[:, sl], vb[:, sl], (((0,), (0,)), ((), ())),
                                  preferred_element_type=f32)
            s_ref[si, h] = dec_t[:, h:h + 1] * s_prev + upd
            on = o * lax.rsqrt(jnp.mean(o * o, axis=-1, keepdims=True) + EPS) * hng[:, sl]
            yb_ref[pl.ds(r0, c), sl] = (on * gate[:, sl]).astype(bf16)

    n_chunks = tb // c
    for si in range(ns):
        if n_chunks == 1:
            chunk(si, si * tb)
        else:
            def body(ci, carry, si=si):
                chunk(si, pl.multiple_of(si * tb + ci * c, c))
                return carry
            lax.fori_loop(0, n_chunks, body, 0)


def _hgrn(z, s0, s0_layer, vecs, n_seq, t, row0, ns, tb, c, layer, states):
    rows = ns * tb
    tpb = t // tb
    base = row0 // rows
    grid = (n_seq // ns, tpb)

    def zspec(colblk):
        return pl.BlockSpec((rows, D_B), lambda n, j: (base + n * tpb + j, colblk))

    vspec = pl.BlockSpec((1, D_B), lambda n, j: (0, 0))
    ins = [z, z, z, z, s0, *vecs, states]
    specs = [zspec(2), zspec(3), zspec(4), zspec(5),
             pl.BlockSpec((None, ns, B_HEADS, B_DK, B_DK), lambda n, j: (s0_layer, n, 0, 0, 0)),
             vspec, vspec, vspec, vspec, pl.BlockSpec(memory_space=pl.ANY)]
    yb, states = pl.pallas_call(
        functools.partial(_hgrn_kernel, ns, tb, c),
        grid=grid,
        in_specs=specs,
        out_specs=[pl.BlockSpec((rows, D_B), lambda n, j: (n * tpb + j, 0)),
                   pl.BlockSpec((None, ns, B_HEADS, B_DK, B_DK), lambda n, j: (layer, n, 0, 0, 0))],
        out_shape=[jax.ShapeDtypeStruct((n_seq * t, D_B), bf16),
                   jax.ShapeDtypeStruct(states.shape, f32)],
        input_output_aliases={len(ins) - 1: 1},
        compiler_params=_params(("arbitrary", "arbitrary")),
        name="hgrn",
    )(*ins)
    return yb, states


def _merge_kernel(prompt_blocks, ya_ref, yb_ref, ga_ref, gb_ref, x_ref, g1p_ref, g1s_ref, wa_ref, wb_ref, wo_ref,
                  n2_ref, shp_ref, shs_ref, scp_ref, scs_ref, rw_ref, rb_ref, xo_ref, h2_ref, lg_ref):
    a = jnp.dot(ya_ref[...], wa_ref[...], preferred_element_type=f32)
    b = jnp.dot(yb_ref[...], wb_ref[...], preferred_element_type=f32)
    merged = _sigmoid(ga_ref[...].astype(f32)) * a + _sigmoid(gb_ref[...].astype(f32)) * b
    mix = jnp.dot(merged.astype(bf16), wo_ref[...], preferred_element_type=f32)
    x = x_ref[...]
    nb, tb, d = x.shape
    x = x + _pick_mod(prompt_blocks, g1p_ref, g1s_ref) * mix.reshape(nb, tb, d)
    xo_ref[...] = x
    hn = x * lax.rsqrt(jnp.mean(x * x, axis=-1, keepdims=True) + EPS) * n2_ref[...]
    hn = hn * (1.0 + _pick_mod(prompt_blocks, scp_ref, scs_ref)) + _pick_mod(prompt_blocks, shp_ref, shs_ref)
    hn = hn.reshape(nb * tb, d)
    h2_ref[...] = hn
    lg_ref[...] = jnp.dot(hn, rw_ref[...], preferred_element_type=f32,
                          precision=lax.Precision.HIGHEST) + rb_ref[...]


def _merge(geo, ya, yb, z, x3, mod, layer, wa, wb, wo, n2g, rw, rb):
    d = D_MODEL
    ga_blk = (2 * D_A + 4 * D_B) // d

    def const(shape):
        return pl.BlockSpec(shape, lambda i: tuple(0 for _ in shape), pipeline_mode=pl.Buffered(1))

    ins = [ya, yb, z, z, x3, mod, mod, wa, wb, wo, n2g.reshape(1, 1, d), mod, mod, mod, mod,
           rw, rb.reshape(1, N_EXPERTS)]
    specs = ([geo.row_spec(D_A), geo.row_spec(D_B), geo.row_spec(d, ga_blk), geo.row_spec(d, ga_blk + 1), geo.x_spec()]
             + geo.mod_specs(layer, 2)
             + [const((D_A, d)), const((D_B, d)), const((d, d)), const((1, 1, d))]
             + geo.mod_specs(layer, 3) + geo.mod_specs(layer, 4)
             + [const((d, N_EXPERTS)), const((1, N_EXPERTS))])
    return pl.pallas_call(
        functools.partial(_merge_kernel, geo.prompt_blocks),
        grid=(geo.n_blocks,),
        in_specs=specs,
        out_specs=[geo.x_spec(), geo.row_spec(d), geo.row_spec(N_EXPERTS)],
        out_shape=[jax.ShapeDtypeStruct(x3.shape, f32), jax.ShapeDtypeStruct((geo.rows, d), f32),
                   jax.ShapeDtypeStruct((geo.rows, N_EXPERTS), f32)],
        compiler_params=_params(("arbitrary",)),
        name="merge",
    )(*ins)


def _up_kernel(layer, nf, be_ref, first_ref, tix_ref, nxt_ref, meta_ref, xs_ref, w_hbm, bg_ref, bl_ref, o_ref,
               stage, wgb_ref, wlb_ref, sem):
    j, i = pl.program_id(0), pl.program_id(1)
    n_tiles = meta_ref[1]

    def tile_copies(e, jj, slot):
        return [pltpu.make_async_copy(
            w_hbm.at[layer, e, :, pl.ds(pl.multiple_of((half * nf + jj) * UP_TF, UP_TF), UP_TF)],
            stage.at[slot, half], sem.at[slot, half]) for half in range(2)]

    @pl.when(first_ref[i] == 1)
    def _():
        q = j * n_tiles + tix_ref[i]
        slot = q % 2

        @pl.when(q == 0)
        def _():
            for cp in tile_copies(be_ref[i], j, slot):
                cp.start()

        for cp in tile_copies(be_ref[i], j, slot):
            cp.wait()
        next_j = jnp.where(tix_ref[i] == n_tiles - 1, j + 1, j)

        @pl.when(next_j < nf)
        def _():
            for cp in tile_copies(nxt_ref[i], next_j, 1 - slot):
                cp.start()

        wgb_ref[...] = stage[slot, 0].astype(bf16)
        wlb_ref[...] = stage[slot, 1].astype(bf16)

    @pl.when(i < meta_ref[0])
    def _():
        x = xs_ref[...].astype(bf16)
        gate = jnp.dot(x, wgb_ref[...], preferred_element_type=f32) + bg_ref[...]
        lin = jnp.dot(x, wlb_ref[...], preferred_element_type=f32) + bl_ref[...]
        gate = jnp.minimum(gate, SWIGLU_LIMIT)
        lin = jnp.clip(lin, -SWIGLU_LIMIT, SWIGLU_LIMIT)
        glu = gate * _sigmoid(SWIGLU_ALPHA * gate)
        o_ref[...] = ((lin + 1.0) * glu).astype(bf16)

    @pl.when(i >= meta_ref[0])
    def _():
        o_ref[...] = jnp.zeros_like(o_ref)


def _down_kernel(layer, be_ref, first_ref, tix_ref, nxt_ref, meta_ref, a_ref, w_hbm, b_ref, rw_ref, o_ref,
                 stage, wb_ref, sem):
    i = pl.program_id(0)

    def tile_copy(e, slot):
        return pltpu.make_async_copy(w_hbm.at[layer, e], stage.at[slot], sem.at[slot])

    @pl.when(first_ref[i] == 1)
    def _():
        q = tix_ref[i]
        slot = q % 2

        @pl.when(q == 0)
        def _():
            tile_copy(be_ref[i], slot).start()

        tile_copy(be_ref[i], slot).wait()

        @pl.when(q + 1 < meta_ref[1])
        def _():
            tile_copy(nxt_ref[i], 1 - slot).start()

        wb_ref[...] = stage[slot].astype(bf16)

    @pl.when(i < meta_ref[0])
    def _():
        y = jnp.dot(a_ref[...], wb_ref[...], preferred_element_type=f32) + b_ref[...]
        o_ref[...] = y * rw_ref[...]

    @pl.when(i >= meta_ref[0])
    def _():
        o_ref[...] = jnp.zeros_like(o_ref)


def _experts(xs, row_w, sched, w_up, b_up, w_down, b_down, layer):
    n_rows = xs.shape[0]
    nb = n_rows // MOE_TM
    nf = D_FF // UP_TF
    act = pl.pallas_call(
        functools.partial(_up_kernel, layer, nf),
        grid_spec=pltpu.PrefetchScalarGridSpec(
            num_scalar_prefetch=5,
            grid=(nf, nb),
            in_specs=[
                pl.BlockSpec((MOE_TM, D_MODEL), lambda j, i, be, *_: (i, 0)),
                pl.BlockSpec(memory_space=pl.ANY),
                pl.BlockSpec((None, None, 1, UP_TF), lambda j, i, be, *_: (layer, be[i], 0, j)),
                pl.BlockSpec((None, None, 1, UP_TF), lambda j, i, be, *_: (layer, be[i], 0, nf + j)),
            ],
            out_specs=pl.BlockSpec((MOE_TM, UP_TF), lambda j, i, be, *_: (i, j)),
            scratch_shapes=[pltpu.VMEM((2, 2, D_MODEL, UP_TF), f32),
                            pltpu.VMEM((D_MODEL, UP_TF), bf16), pltpu.VMEM((D_MODEL, UP_TF), bf16),
                            pltpu.SemaphoreType.DMA((2, 2))],
        ),
        out_shape=jax.ShapeDtypeStruct((n_rows, D_FF), bf16),
        compiler_params=_params(("arbitrary", "arbitrary")),
        name="moe_up",
    )(*sched, xs, w_up, b_up, b_up)
    return pl.pallas_call(
        functools.partial(_down_kernel, layer),
        grid_spec=pltpu.PrefetchScalarGridSpec(
            num_scalar_prefetch=5,
            grid=(nb,),
            in_specs=[
                pl.BlockSpec((MOE_TM, D_FF), lambda i, be, *_: (i, 0)),
                pl.BlockSpec(memory_space=pl.ANY),
                pl.BlockSpec((None, None, 1, D_MODEL), lambda i, be, *_: (layer, be[i], 0, 0)),
                pl.BlockSpec((MOE_TM, 1), lambda i, be, *_: (i, 0)),
            ],
            out_specs=pl.BlockSpec((MOE_TM, D_MODEL), lambda i, be, *_: (i, 0)),
            scratch_shapes=[pltpu.VMEM((2, D_FF, D_MODEL), f32), pltpu.VMEM((D_FF, D_MODEL), bf16),
                            pltpu.SemaphoreType.DMA((2,))],
        ),
        out_shape=jax.ShapeDtypeStruct((n_rows, D_MODEL), f32),
        compiler_params=_params(("arbitrary",)),
        name="moe_down",
    )(*sched, act, w_down, b_down, row_w)


def _route(logits):
    t = logits.shape[0]
    tk = t * TOP_K
    lane = jnp.arange(N_EXPERTS, dtype=jnp.int32)[None, :]
    rest, idx, val = logits, [], []
    for _ in range(TOP_K):
        i = jnp.argmax(rest, axis=1).astype(jnp.int32)
        idx.append(i)
        val.append(jnp.max(rest, axis=1))
        rest = jnp.where(lane == i[:, None], -jnp.inf, rest)
    top_e = jnp.stack(idx, axis=1)
    weights = jax.nn.softmax(jnp.stack(val, axis=1), axis=-1)
    member = sum((lane == i[:, None]).astype(jnp.int32) for i in idx)
    csum = jnp.cumsum(member, axis=0)
    counts = csum[-1]
    rank = csum - member
    padded = (counts + MOE_TM - 1) // MOE_TM * MOE_TM
    pad_end = jnp.cumsum(padded)
    pad_start = pad_end - padded
    start = jnp.cumsum(counts) - counts
    slot_row = pad_start[None, :] + rank
    pos = jnp.stack([jnp.sum(jnp.where(lane == i[:, None], slot_row, 0), axis=1) for i in idx], axis=1)
    n_blocks = tk // MOE_TM + N_EXPERTS
    n_rows = n_blocks * MOE_TM
    blk = jnp.arange(n_blocks, dtype=jnp.int32)
    blk_start = blk * MOE_TM
    block_e = jnp.minimum(jnp.sum(pad_end[None, :] <= blk_start[:, None], axis=1), N_EXPERTS - 1).astype(jnp.int32)
    n_valid = (pad_end[-1] // MOE_TM).astype(jnp.int32)
    first = ((blk_start == pad_start[block_e]) & (blk_start < pad_end[-1])).astype(jnp.int32)
    tix = jnp.maximum(jnp.cumsum(first) - 1, 0).astype(jnp.int32)
    first_pos = jnp.where(first == 1, blk, n_blocks)
    next_pos = jnp.concatenate([lax.cummin(first_pos, reverse=True)[1:], jnp.full((1,), n_blocks, jnp.int32)])
    nxt = jnp.where(next_pos < n_blocks, block_e[jnp.minimum(next_pos, n_blocks - 1)], block_e[0]).astype(jnp.int32)
    meta = jnp.stack([n_valid, jnp.sum(first).astype(jnp.int32)])
    order = jnp.argsort(top_e.reshape(tk), stable=True).astype(jnp.int32)
    row = jnp.arange(n_rows, dtype=jnp.int32)
    row_e = jnp.repeat(block_e, MOE_TM)
    offset = row - pad_start[row_e]
    live = offset < counts[row_e]
    slot = order[jnp.clip(start[row_e] + offset, 0, tk - 1)]
    row_tok = jnp.where(live, slot // TOP_K, 0)
    row_w = jnp.where(live, weights.reshape(tk)[slot], 0.0)
    return row_tok, row_w.reshape(n_rows, 1), (block_e, first, tix, nxt, meta), pos.astype(jnp.int32)


def kernel(x_prompt, x_sample, c_prompt, c_sample, state_hgrn, w_mod, b_mod, norm1_g, w_in, gmlp_ln_g, gmlp_ln_b, gmlp_w_s, gmlp_b_s, hgrn_lb_raw, hgrn_norm_g, w_branch_a, w_branch_b, w_out, norm2_g, router_w, router_b, exp_w_up, exp_b_up, exp_w_down, exp_b_down, final_norm_g):
    n_p, t_p, d = x_prompt.shape
    n_s, t_s, _ = x_sample.shape
    rows_p, rows_s = n_p * t_p, n_s * t_s
    geo = _Rows(n_p, t_p, n_s, t_s)

    n_c = n_p + n_s
    n_c_pad = -(-n_c // 8) * 8
    c_all = jnp.concatenate([c_sample, c_prompt, jnp.zeros((n_c_pad - n_c, d), f32)], axis=0)
    mod = _modulation(c_all, w_mod, b_mod).reshape(DEPTH, n_c_pad, 1, N_MOD * d)

    p = jax.nn.softmax(hgrn_lb_raw.astype(f32), axis=0)
    lower = jnp.maximum(jnp.cumsum(p, axis=0) - p[0:1], 0.0)

    s0_prompt = jnp.zeros((1, n_p, B_HEADS, B_DK, B_DK), f32)
    states_p = jnp.zeros((DEPTH, n_p, B_HEADS, B_DK, B_DK), f32)
    states_s = jnp.zeros((DEPTH, n_s, B_HEADS, B_DK, B_DK), f32)
    b_up4 = exp_b_up.reshape(DEPTH, N_EXPERTS, 1, 2 * D_FF)
    b_down4 = exp_b_down.reshape(DEPTH, N_EXPERTS, 1, D_MODEL)
    reps = CHUNK // t_s

    x = jnp.concatenate([x_prompt.reshape(rows_p // SEG, SEG, d), x_sample], axis=0)
    moe_out = None
    v_rows = []
    for l in range(DEPTH):
        if l == 0:
            _, h = _norm(geo, x, norm1_g[l], mod=mod, mod_layer=l)
        else:
            x, h = _norm(geo, x, norm1_g[l], y=moe_out, mod=mod, res_layer=l - 1, mod_layer=l)
        z = _in_proj(h, w_in, l)

        ws = gmlp_w_s[l]
        w_mix = jnp.stack([ws, jnp.tile(ws[:, :t_s, :t_s], (1, reps, reps))])
        bs = gmlp_b_s[l]
        b_mix = jnp.stack([jnp.repeat(bs.T, A_GW, axis=1),
                           jnp.repeat(jnp.tile(bs[:, :t_s].T, (reps, 1)), A_GW, axis=1)])
        ya, vn = _gmlp(z, w_mix, b_mix, gmlp_ln_g[l].reshape(1, D_A), gmlp_ln_b[l].reshape(1, D_A), rows_p, t_s)
        v_rows.append(vn[rows_p:].reshape(n_s, t_s, D_A))

        lb = lower[l].reshape(1, D_B)
        vecs = (jnp.log(lb), jnp.log1p(-lb), 1.0 - lb, hgrn_norm_g[l].reshape(1, D_B).astype(f32))
        yb_p, states_p = _hgrn(z, s0_prompt, 0, vecs, n_p, t_p, 0, 1, HGRN_TB, HGRN_C, l, states_p)
        yb_s, states_s = _hgrn(z, state_hgrn, l, vecs, n_s, t_s, rows_p, HGRN_NS, t_s, t_s, l, states_s)
        yb = jnp.concatenate([yb_p, yb_s], axis=0)

        wa, wb, wo = w_branch_a[l].astype(bf16), w_branch_b[l].astype(bf16), w_out[l].astype(bf16)
        x, h2, logits = _merge(geo, ya, yb, z, x, mod, l, wa, wb, wo, norm2_g[l], router_w[l], router_b[l])

        row_tok, row_w, sched, pos = _route(logits)
        xs = h2[row_tok]
        y_rows = _experts(xs, row_w, sched, exp_w_up, b_up4, exp_w_down, b_down4, l)
        moe_out = y_rows[pos.T.reshape(-1)].reshape(TOP_K, geo.rows, d)

    _, y = _norm(geo, x, final_norm_g, y=moe_out, mod=mod, res_layer=DEPTH - 1, out_dtype=f32)
    return (y[:rows_p].reshape(n_p, t_p, d), y[rows_p:].reshape(n_s, t_s, d),
            states_p, states_s, jnp.stack(v_rows))
```

```python
import functools

import jax
import jax.numpy as jnp
from jax import lax
from jax.experimental import pallas as pl
from jax.experimental.pallas import tpu as pltpu

f32 = jnp.float32
bf16 = jnp.bfloat16

D_MODEL = 2048
DEPTH = 4
D_A = D_MODEL // 2
A_GROUPS = 8
A_GW = D_A // A_GROUPS
CHUNK = 128
D_B = D_MODEL // 2
B_DK = 128
B_HEADS = D_B // B_DK
IN_W = 2 * D_A + 4 * D_B + 2 * D_MODEL
N_EXPERTS = 32
TOP_K = 4
D_FF = D_MODEL
SWIGLU_LIMIT = 7.0
SWIGLU_ALPHA = 1.702
N_MOD = 6
EPS = 1e-6
LOG2_E = 1.4426950408889634

VMEM_LIMIT_BYTES = 56 * 1024 * 1024

SEG = 8
ROW_BLOCK = 256
SEGS = ROW_BLOCK // SEG
MERGE_SPLIT = 2
IN_TM = 1024
IN_TN = 1024
MOE_TM = 256
UP_TF = 1024
HGRN_TB = 256
HGRN_C = 16
HGRN_NS = 8


def _params(sem):
    return pltpu.CompilerParams(dimension_semantics=sem, vmem_limit_bytes=VMEM_LIMIT_BYTES)


def _sigmoid(x):
    return jax.nn.sigmoid(x)


def _gelu(x):
    return 0.5 * x * (1.0 + lax.erf(x * 0.7071067811865476))


def _mod_kernel(c_ref, w_ref, b_ref, o_ref):
    c = c_ref[...]
    a = (c * _sigmoid(c)).astype(bf16)
    o_ref[...] = jnp.dot(a, w_ref[...].astype(bf16), preferred_element_type=f32) + b_ref[...]


def _modulation(c_all, w_mod, b_mod):
    n = c_all.shape[0]
    tn = 1024
    return pl.pallas_call(
        _mod_kernel,
        grid=(DEPTH, N_MOD * D_MODEL // tn),
        in_specs=[
            pl.BlockSpec((n, D_MODEL), lambda l, j: (0, 0)),
            pl.BlockSpec((None, D_MODEL, tn), lambda l, j: (l, 0, j)),
            pl.BlockSpec((None, 1, tn), lambda l, j: (l, 0, j)),
        ],
        out_specs=pl.BlockSpec((None, n, tn), lambda l, j: (l, 0, j)),
        out_shape=jax.ShapeDtypeStruct((DEPTH, n, N_MOD * D_MODEL), f32),
        compiler_params=_params(("arbitrary", "arbitrary")),
        name="modulation",
    )(c_all, w_mod, b_mod.reshape(DEPTH, 1, N_MOD * D_MODEL))


class _Rows:
    def __init__(self, n_p, t_p, n_s, t_s):
        assert t_s == SEG and t_p % ROW_BLOCK == 0 and (n_s * t_s) % ROW_BLOCK == 0
        self.n_p, self.n_s = n_p, n_s
        self.blocks_per_seq = t_p // ROW_BLOCK
        self.prompt_blocks = n_p * self.blocks_per_seq
        self.n_blocks = self.prompt_blocks + n_s * t_s // ROW_BLOCK
        self.rows = self.n_blocks * ROW_BLOCK

    def x_spec(self):
        return pl.BlockSpec((SEGS, SEG, D_MODEL), lambda i: (i, 0, 0))

    def row_spec(self, width, colblk=0):
        return pl.BlockSpec((ROW_BLOCK, width), lambda i: (i, colblk))

    def mod_specs(self, layer, col):
        pb, bps, n_p, n_s = self.prompt_blocks, self.blocks_per_seq, self.n_p, self.n_s
        return [pl.BlockSpec((None, 1, 1, D_MODEL), lambda i: (layer, n_s + jnp.minimum(i // bps, n_p - 1), 0, col)),
                pl.BlockSpec((None, SEGS, 1, D_MODEL), lambda i: (layer, jnp.maximum(i - pb, 0), 0, col))]


def _pick_mod(prompt_blocks, p_ref, s_ref):
    return jnp.where(pl.program_id(0) < prompt_blocks, p_ref[...], s_ref[...])


def _norm_kernel(prompt_blocks, has_res, has_mod, *refs):
    refs = list(refs)
    x_ref = refs.pop(0)
    if has_res:
        y_ref, gp_ref, gs_ref = refs.pop(0), refs.pop(0), refs.pop(0)
    g_ref = refs.pop(0)
    if has_mod:
        shp_ref, shs_ref, scp_ref, scs_ref = (refs.pop(0) for _ in range(4))
    if has_res:
        xo_ref = refs.pop(0)
    h_ref = refs.pop(0)

    x = x_ref[...]
    nb, tb, d = x.shape
    if has_res:
        y = y_ref[0]
        for k in range(1, y_ref.shape[0]):
            y = y + y_ref[k]
        x = x + _pick_mod(prompt_blocks, gp_ref, gs_ref) * y.reshape(nb, tb, d)
        xo_ref[...] = x
    hn = x * lax.rsqrt(jnp.mean(x * x, axis=-1, keepdims=True) + EPS) * g_ref[...]
    if has_mod:
        hn = hn * (1.0 + _pick_mod(prompt_blocks, scp_ref, scs_ref)) + _pick_mod(prompt_blocks, shp_ref, shs_ref)
    h_ref[...] = hn.reshape(nb * tb, d).astype(h_ref.dtype)


def _norm(geo, x3, g, *, y=None, mod=None, res_layer=None, mod_layer=None, out_dtype=bf16):
    d = D_MODEL
    has_res, has_mod = y is not None, mod_layer is not None
    ins, specs = [x3], [geo.x_spec()]
    if has_res:
        ins += [y, mod, mod]
        specs += [pl.BlockSpec((y.shape[0], ROW_BLOCK, d), lambda i: (0, i, 0))] + geo.mod_specs(res_layer, 5)
    ins.append(g.reshape(1, 1, d))
    specs.append(pl.BlockSpec((1, 1, d), lambda i: (0, 0, 0)))
    if has_mod:
        ins += [mod] * 4
        specs += geo.mod_specs(mod_layer, 0) + geo.mod_specs(mod_layer, 1)
    out_shapes, out_specs = [], []
    if has_res:
        out_shapes.append(jax.ShapeDtypeStruct(x3.shape, f32))
        out_specs.append(geo.x_spec())
    out_shapes.append(jax.ShapeDtypeStruct((geo.rows, d), out_dtype))
    out_specs.append(geo.row_spec(d))
    res = pl.pallas_call(
        functools.partial(_norm_kernel, geo.prompt_blocks, has_res, has_mod),
        grid=(geo.n_blocks,), in_specs=specs, out_specs=out_specs, out_shape=out_shapes,
        compiler_params=_params(("arbitrary",)),
        name="norm",
    )(*ins)
    return res if has_res else (None, res[0])


def _in_kernel(h_ref, w_ref, o_ref, wb_ref):
    @pl.when(pl.program_id(1) == 0)
    def _():
        wb_ref[...] = w_ref[...].astype(bf16)

    o_ref[...] = jnp.dot(h_ref[...], wb_ref[...], preferred_element_type=f32).astype(bf16)


def _in_proj(h, w_in, layer):
    rows = h.shape[0]
    return pl.pallas_call(
        _in_kernel,
        grid=(IN_W // IN_TN, rows // IN_TM),
        in_specs=[
            pl.BlockSpec((IN_TM, D_MODEL), lambda j, m: (m, 0)),
            pl.BlockSpec((None, D_MODEL, IN_TN), lambda j, m: (layer, 0, j)),
        ],
        out_specs=pl.BlockSpec((IN_TM, IN_TN), lambda j, m: (m, j)),
        out_shape=jax.ShapeDtypeStruct((rows, IN_W), bf16),
        scratch_shapes=[pltpu.VMEM((D_MODEL, IN_TN), bf16)],
        compiler_params=_params(("arbitrary", "arbitrary")),
        name="in_proj",
    )(h, w_in)


def _gmlp_kernel(n_prompt_chunks, sample_shift, u_ref, v_ref, w_ref, b_ref, lg_ref, lb_ref, ya_ref, vn_ref):
    c = pl.program_id(0)
    u = _gelu(u_ref[...].astype(f32))
    v = _gelu(v_ref[...].astype(f32))
    vc = v - jnp.mean(v, axis=-1, keepdims=True)
    vn = vc * lax.rsqrt(jnp.mean(vc * vc, axis=-1, keepdims=True) + EPS) * lg_ref[...] + lb_ref[...]
    vn_ref[...] = vn
    shift = jnp.where(c >= n_prompt_chunks, sample_shift, 7)
    row = lax.broadcasted_iota(jnp.int32, (CHUNK, CHUNK), 0)
    col = lax.broadcasted_iota(jnp.int32, (CHUNK, CHUNK), 1)
    mask = ((row >> shift) == (col >> shift)) & (col <= row)
    vnb = vn.astype(bf16)
    mixed = []
    for g in range(A_GROUPS):
        wg = jnp.where(mask, w_ref[g], 0.0).astype(bf16)
        mixed.append(jnp.dot(wg, vnb[:, g * A_GW:(g + 1) * A_GW], preferred_element_type=f32))
    mixed = jnp.concatenate(mixed, axis=1) + b_ref[...]
    ya_ref[...] = (u * mixed).astype(bf16)


def _gmlp(z, w_mix, b_mix, ln_g, ln_b, rows_p, t_s):
    rows = z.shape[0]
    ncp = rows_p // CHUNK
    return pl.pallas_call(
        functools.partial(_gmlp_kernel, ncp, t_s.bit_length() - 1),
        grid=(rows // CHUNK,),
        in_specs=[
            pl.BlockSpec((CHUNK, D_A), lambda c: (c, 0)),
            pl.BlockSpec((CHUNK, D_A), lambda c: (c, 1)),
            pl.BlockSpec((None, A_GROUPS, CHUNK, CHUNK), lambda c: (jnp.where(c >= ncp, 1, 0), 0, 0, 0)),
            pl.BlockSpec((None, CHUNK, D_A), lambda c: (jnp.where(c >= ncp, 1, 0), 0, 0)),
            pl.BlockSpec((1, D_A), lambda c: (0, 0)),
            pl.BlockSpec((1, D_A), lambda c: (0, 0)),
        ],
        out_specs=[pl.BlockSpec((CHUNK, D_A), lambda c: (c, 0)), pl.BlockSpec((CHUNK, D_A), lambda c: (c, 0))],
        out_shape=[jax.ShapeDtypeStruct((rows, D_A), bf16), jax.ShapeDtypeStruct((rows, D_A), f32)],
        compiler_params=_params(("arbitrary",)),
        name="gmlp",
    )(z, z, w_mix, b_mix, ln_g, ln_b)


def _hgrn_kernel(ns, tb, c, q_ref, f_ref, i_ref, g_ref, s0_ref, llb_ref, l1m_ref, oml_ref, hng_ref, all_ref,
                 yb_ref, s_ref):
    del all_ref

    @pl.when(pl.program_id(1) == 0)
    def _():
        s_ref[...] = s0_ref[...]

    row = lax.broadcasted_iota(jnp.int32, (c, c), 0)
    col = lax.broadcasted_iota(jnp.int32, (c, c), 1)
    tri = (col <= row).astype(f32)
    rows1 = lax.broadcasted_iota(jnp.int32, (c, 1), 0)
    ones = jnp.ones((B_DK, B_DK), bf16)
    llb, l1m, oml, hng = llb_ref[...], l1m_ref[...], oml_ref[...], hng_ref[...]

    def chunk(si, r0):
        zq = q_ref[pl.ds(r0, c), :].astype(f32)
        zf = f_ref[pl.ds(r0, c), :].astype(f32)
        zi = i_ref[pl.ds(r0, c), :].astype(f32)
        zg = g_ref[pl.ds(r0, c), :].astype(f32)
        qs = zq * _sigmoid(zq)
        a = l1m + jnp.minimum(zf, 0.0) - jnp.log1p(jnp.exp(-jnp.abs(zf)))
        logf = jnp.maximum(a, llb) + jnp.log1p(jnp.exp(-jnp.abs(a - llb)))
        kk = oml * _sigmoid(-zf)
        gc = jnp.dot(tri, logf, preferred_element_type=f32, precision=lax.Precision.HIGHEST)
        gc2 = gc * LOG2_E
        g_last = gc[c - 1:c, :]
        q_dec = (qs * jnp.exp(gc)).astype(bf16)
        k_dec = (kk * jnp.exp(g_last - gc)).astype(bf16)
        vb = zi.astype(bf16)
        dec = jnp.exp(jnp.concatenate([g_last[:, h * B_DK:(h + 1) * B_DK] for h in range(B_HEADS)], axis=0))
        dec_t = dec.T
        gate = zg * _sigmoid(zg)
        for h in range(B_HEADS):
            sl = slice(h * B_DK, (h + 1) * B_DK)
            s_prev = s_ref[si, h]
            o = jnp.dot(q_dec[:, sl], s_prev.astype(bf16), preferred_element_type=f32)
            qh, gh, kh, vh = qs[:, sl], gc2[:, sl], kk[:, sl], zi[:, sl]
            terms = []
            for s in range(c):
                e = jnp.where(rows1 >= s, jnp.exp2(gh - gh[s:s + 1, :]), 0.0)
                terms.append(qh * (kh[s:s + 1, :] * e))
            a_rows = jnp.dot(jnp.concatenate(terms, axis=0).astype(bf16), ones, preferred_element_type=f32)
            for s in range(c):
                o = o + a_rows[s * c:(s + 1) * c, :] * vh[s:s + 1, :]
            upd = lax.dot_general(k_dec[:, sl], vb[:, sl], (((0,), (0,)), ((), ())),
                                  preferred_element_type=f32)
            s_ref[si, h] = dec_t[:, h:h + 1] * s_prev + upd
            on = o * lax.rsqrt(jnp.mean(o * o, axis=-1, keepdims=True) + EPS) * hng[:, sl]
            yb_ref[pl.ds(r0, c), sl] = (on * gate[:, sl]).astype(bf16)

    n_chunks = tb // c
    for si in range(ns):
        if n_chunks == 1:
            chunk(si, si * tb)
        else:
            def body(ci, carry, si=si):
                chunk(si, pl.multiple_of(si * tb + ci * c, c))
                return carry
            lax.fori_loop(0, n_chunks, body, 0, unroll=2)


def _hgrn(z, s0, s0_layer, vecs, n_seq, t, row0, ns, tb, c, layer, states):
    rows = ns * tb
    tpb = t // tb
    base = row0 // rows
    grid = (n_seq // ns, tpb)

    def zspec(colblk):
        return pl.BlockSpec((rows, D_B), lambda n, j: (base + n * tpb + j, colblk))

    vspec = pl.BlockSpec((1, D_B), lambda n, j: (0, 0))
    ins = [z, z, z, z, s0, *vecs, states]
    specs = [zspec(2), zspec(3), zspec(4), zspec(5),
             pl.BlockSpec((None, ns, B_HEADS, B_DK, B_DK), lambda n, j: (s0_layer, n, 0, 0, 0)),
             vspec, vspec, vspec, vspec, pl.BlockSpec(memory_space=pl.ANY)]
    yb, states = pl.pallas_call(
        functools.partial(_hgrn_kernel, ns, tb, c),
        grid=grid,
        in_specs=specs,
        out_specs=[pl.BlockSpec((rows, D_B), lambda n, j: (n * tpb + j, 0)),
                   pl.BlockSpec((None, ns, B_HEADS, B_DK, B_DK), lambda n, j: (layer, n, 0, 0, 0))],
        out_shape=[jax.ShapeDtypeStruct((n_seq * t, D_B), bf16),
                   jax.ShapeDtypeStruct(states.shape, f32)],
        input_output_aliases={len(ins) - 1: 1},
        compiler_params=_params(("arbitrary", "arbitrary")),
        name="hgrn",
    )(*ins)
    return yb, states


def _merge_kernel(prompt_blocks, ya_ref, yb_ref, ga_ref, gb_ref, x_ref, g1p_ref, g1s_ref, wa_ref, wb_ref, wo_ref,
                  n2_ref, shp_ref, shs_ref, scp_ref, scs_ref, rw_ref, rb_ref, xo_ref, h2_ref, lg_ref):
    gate1 = _pick_mod(prompt_blocks, g1p_ref, g1s_ref)
    scale2 = _pick_mod(prompt_blocks, scp_ref, scs_ref)
    shift2 = _pick_mod(prompt_blocks, shp_ref, shs_ref)
    d = D_MODEL
    segs = SEGS // MERGE_SPLIT
    rows = segs * SEG
    for part in range(MERGE_SPLIT):
        r = slice(part * rows, (part + 1) * rows)
        sg = slice(part * segs, (part + 1) * segs)
        a = jnp.dot(ya_ref[r, :], wa_ref[...], preferred_element_type=f32)
        b = jnp.dot(yb_ref[r, :], wb_ref[...], preferred_element_type=f32)
        merged = _sigmoid(ga_ref[r, :].astype(f32)) * a + _sigmoid(gb_ref[r, :].astype(f32)) * b
        mix = jnp.dot(merged.astype(bf16), wo_ref[...], preferred_element_type=f32)
        x = x_ref[sg] + gate1[sg] * mix.reshape(segs, SEG, d)
        xo_ref[sg] = x
        hn = x * lax.rsqrt(jnp.mean(x * x, axis=-1, keepdims=True) + EPS) * n2_ref[...]
        hn = (hn * (1.0 + scale2[sg]) + shift2[sg]).reshape(rows, d)
        h2_ref[r, :] = hn
        hi = hn.astype(bf16)
        lo = (hn - hi.astype(f32)).astype(bf16)
        hh_hl = jnp.dot(hi, rw_ref[...], preferred_element_type=f32)
        lh = jnp.dot(lo, rw_ref[:, :N_EXPERTS], preferred_element_type=f32)
        lg_ref[r, :] = hh_hl[:, :N_EXPERTS] + hh_hl[:, N_EXPERTS:] + lh + rb_ref[...]


def _merge(geo, ya, yb, z, x3, mod, layer, wa, wb, wo, n2g, rw, rb):
    d = D_MODEL
    ga_blk = (2 * D_A + 4 * D_B) // d

    def const(shape):
        return pl.BlockSpec(shape, lambda i: tuple(0 for _ in shape), pipeline_mode=pl.Buffered(1))

    rw_hi = rw.astype(bf16)
    rw_split = jnp.concatenate([rw_hi, (rw - rw_hi.astype(f32)).astype(bf16)], axis=1)
    ins = [ya, yb, z, z, x3, mod, mod, wa, wb, wo, n2g.reshape(1, 1, d), mod, mod, mod, mod,
           rw_split, rb.reshape(1, N_EXPERTS)]
    specs = ([geo.row_spec(D_A), geo.row_spec(D_B), geo.row_spec(d, ga_blk), geo.row_spec(d, ga_blk + 1), geo.x_spec()]
             + geo.mod_specs(layer, 2)
             + [const((D_A, d)), const((D_B, d)), const((d, d)), const((1, 1, d))]
             + geo.mod_specs(layer, 3) + geo.mod_specs(layer, 4)
             + [const((d, 2 * N_EXPERTS)), const((1, N_EXPERTS))])
    return pl.pallas_call(
        functools.partial(_merge_kernel, geo.prompt_blocks),
        grid=(geo.n_blocks,),
        in_specs=specs,
        out_specs=[geo.x_spec(), geo.row_spec(d), geo.row_spec(N_EXPERTS)],
        out_shape=[jax.ShapeDtypeStruct(x3.shape, f32), jax.ShapeDtypeStruct((geo.rows, d), f32),
                   jax.ShapeDtypeStruct((geo.rows, N_EXPERTS), f32)],
        compiler_params=_params(("arbitrary",)),
        name="merge",
    )(*ins)


def _up_kernel(layer, nf, be_ref, first_ref, tix_ref, nxt_ref, meta_ref, xs_ref, w_hbm, bg_ref, bl_ref, o_ref,
               stage, wgb_ref, wlb_ref, sem):
    j, i = pl.program_id(0), pl.program_id(1)
    n_tiles = meta_ref[1]

    def tile_copies(e, jj, slot):
        return [pltpu.make_async_copy(
            w_hbm.at[layer, e, :, pl.ds(pl.multiple_of((half * nf + jj) * UP_TF, UP_TF), UP_TF)],
            stage.at[slot, half], sem.at[slot, half]) for half in range(2)]

    @pl.when(first_ref[i] == 1)
    def _():
        q = j * n_tiles + tix_ref[i]
        slot = q % 2

        @pl.when(q == 0)
        def _():
            for cp in tile_copies(be_ref[i], j, slot):
                cp.start()

        for cp in tile_copies(be_ref[i], j, slot):
            cp.wait()
        next_j = jnp.where(tix_ref[i] == n_tiles - 1, j + 1, j)

        @pl.when(next_j < nf)
        def _():
            for cp in tile_copies(nxt_ref[i], next_j, 1 - slot):
                cp.start()

        wgb_ref[...] = stage[slot, 0].astype(bf16)
        wlb_ref[...] = stage[slot, 1].astype(bf16)

    @pl.when(i < meta_ref[0])
    def _():
        x = xs_ref[...].astype(bf16)
        gate = jnp.dot(x, wgb_ref[...], preferred_element_type=f32) + bg_ref[...]
        lin = jnp.dot(x, wlb_ref[...], preferred_element_type=f32) + bl_ref[...]
        gate = jnp.minimum(gate, SWIGLU_LIMIT)
        lin = jnp.clip(lin, -SWIGLU_LIMIT, SWIGLU_LIMIT)
        glu = gate * _sigmoid(SWIGLU_ALPHA * gate)
        o_ref[...] = ((lin + 1.0) * glu).astype(bf16)

    @pl.when(i >= meta_ref[0])
    def _():
        o_ref[...] = jnp.zeros_like(o_ref)


def _down_kernel(layer, be_ref, first_ref, tix_ref, nxt_ref, meta_ref, a_ref, w_hbm, b_ref, rw_ref, o_ref,
                 stage, wb_ref, sem):
    i = pl.program_id(0)

    def tile_copy(e, slot):
        return pltpu.make_async_copy(w_hbm.at[layer, e], stage.at[slot], sem.at[slot])

    @pl.when(first_ref[i] == 1)
    def _():
        q = tix_ref[i]
        slot = q % 2

        @pl.when(q == 0)
        def _():
            tile_copy(be_ref[i], slot).start()

        tile_copy(be_ref[i], slot).wait()

        @pl.when(q + 1 < meta_ref[1])
        def _():
            tile_copy(nxt_ref[i], 1 - slot).start()

        wb_ref[...] = stage[slot].astype(bf16)

    @pl.when(i < meta_ref[0])
    def _():
        y = jnp.dot(a_ref[...], wb_ref[...], preferred_element_type=f32) + b_ref[...]
        o_ref[...] = y * rw_ref[...]

    @pl.when(i >= meta_ref[0])
    def _():
        o_ref[...] = jnp.zeros_like(o_ref)


def _experts(xs, row_w, sched, w_up, b_up, w_down, b_down, layer):
    n_rows = xs.shape[0]
    nb = n_rows // MOE_TM
    nf = D_FF // UP_TF
    act = pl.pallas_call(
        functools.partial(_up_kernel, layer, nf),
        grid_spec=pltpu.PrefetchScalarGridSpec(
            num_scalar_prefetch=5,
            grid=(nf, nb),
            in_specs=[
                pl.BlockSpec((MOE_TM, D_MODEL), lambda j, i, be, *_: (i, 0)),
                pl.BlockSpec(memory_space=pl.ANY),
                pl.BlockSpec((None, None, 1, UP_TF), lambda j, i, be, *_: (layer, be[i], 0, j)),
                pl.BlockSpec((None, None, 1, UP_TF), lambda j, i, be, *_: (layer, be[i], 0, nf + j)),
            ],
            out_specs=pl.BlockSpec((MOE_TM, UP_TF), lambda j, i, be, *_: (i, j)),
            scratch_shapes=[pltpu.VMEM((2, 2, D_MODEL, UP_TF), f32),
                            pltpu.VMEM((D_MODEL, UP_TF), bf16), pltpu.VMEM((D_MODEL, UP_TF), bf16),
                            pltpu.SemaphoreType.DMA((2, 2))],
        ),
        out_shape=jax.ShapeDtypeStruct((n_rows, D_FF), bf16),
        compiler_params=_params(("arbitrary", "arbitrary")),
        name="moe_up",
    )(*sched, xs, w_up, b_up, b_up)
    return pl.pallas_call(
        functools.partial(_down_kernel, layer),
        grid_spec=pltpu.PrefetchScalarGridSpec(
            num_scalar_prefetch=5,
            grid=(nb,),
            in_specs=[
                pl.BlockSpec((MOE_TM, D_FF), lambda i, be, *_: (i, 0)),
                pl.BlockSpec(memory_space=pl.ANY),
                pl.BlockSpec((None, None, 1, D_MODEL), lambda i, be, *_: (layer, be[i], 0, 0)),
                pl.BlockSpec((MOE_TM, 1), lambda i, be, *_: (i, 0)),
            ],
            out_specs=pl.BlockSpec((MOE_TM, D_MODEL), lambda i, be, *_: (i, 0)),
            scratch_shapes=[pltpu.VMEM((2, D_FF, D_MODEL), f32), pltpu.VMEM((D_FF, D_MODEL), bf16),
                            pltpu.SemaphoreType.DMA((2,))],
        ),
        out_shape=jax.ShapeDtypeStruct((n_rows, D_MODEL), f32),
        compiler_params=_params(("arbitrary",)),
        name="moe_down",
    )(*sched, act, w_down, b_down, row_w)


def _route(logits):
    t = logits.shape[0]
    tk = t * TOP_K
    lane = jnp.arange(N_EXPERTS, dtype=jnp.int32)[None, :]
    rest, idx, val = logits, [], []
    for _ in range(TOP_K):
        i = jnp.argmax(rest, axis=1).astype(jnp.int32)
        idx.append(i)
        val.append(jnp.max(rest, axis=1))
        rest = jnp.where(lane == i[:, None], -jnp.inf, rest)
    top_e = jnp.stack(idx, axis=1)
    weights = jax.nn.softmax(jnp.stack(val, axis=1), axis=-1)
    member = sum((lane == i[:, None]).astype(jnp.int32) for i in idx)
    csum = jnp.cumsum(member, axis=0)
    counts = csum[-1]
    rank = csum - member
    padded = (counts + MOE_TM - 1) // MOE_TM * MOE_TM
    pad_end = jnp.cumsum(padded)
    pad_start = pad_end - padded
    start = jnp.cumsum(counts) - counts
    slot_row = pad_start[None, :] + rank
    pos = jnp.stack([jnp.sum(jnp.where(lane == i[:, None], slot_row, 0), axis=1) for i in idx], axis=1)
    n_blocks = tk // MOE_TM + N_EXPERTS
    n_rows = n_blocks * MOE_TM
    blk = jnp.arange(n_blocks, dtype=jnp.int32)
    blk_start = blk * MOE_TM
    block_e = jnp.minimum(jnp.sum(pad_end[None, :] <= blk_start[:, None], axis=1), N_EXPERTS - 1).astype(jnp.int32)
    n_valid = (pad_end[-1] // MOE_TM).astype(jnp.int32)
    first = ((blk_start == pad_start[block_e]) & (blk_start < pad_end[-1])).astype(jnp.int32)
    tix = jnp.maximum(jnp.cumsum(first) - 1, 0).astype(jnp.int32)
    first_pos = jnp.where(first == 1, blk, n_blocks)
    next_pos = jnp.concatenate([lax.cummin(first_pos, reverse=True)[1:], jnp.full((1,), n_blocks, jnp.int32)])
    nxt = jnp.where(next_pos < n_blocks, block_e[jnp.minimum(next_pos, n_blocks - 1)], block_e[0]).astype(jnp.int32)
    meta = jnp.stack([n_valid, jnp.sum(first).astype(jnp.int32)])
    order = jnp.argsort(top_e.reshape(tk), stable=True).astype(jnp.int32)
    in_blk = jnp.arange(MOE_TM, dtype=jnp.int32)[None, :]
    offset = (blk_start - pad_start[block_e])[:, None] + in_blk
    live = (offset < counts[block_e][:, None]).reshape(n_rows)
    slot = order[jnp.clip(start[block_e][:, None] + offset, 0, tk - 1).reshape(n_rows)]
    row_tok = jnp.where(live, slot // TOP_K, 0)
    row_w = jnp.where(live, weights.reshape(tk)[slot], 0.0)
    return row_tok, row_w.reshape(n_rows, 1), (block_e, first, tix, nxt, meta), pos.astype(jnp.int32)


def kernel(x_prompt, x_sample, c_prompt, c_sample, state_hgrn, w_mod, b_mod, norm1_g, w_in, gmlp_ln_g, gmlp_ln_b, gmlp_w_s, gmlp_b_s, hgrn_lb_raw, hgrn_norm_g, w_branch_a, w_branch_b, w_out, norm2_g, router_w, router_b, exp_w_up, exp_b_up, exp_w_down, exp_b_down, final_norm_g):
    n_p, t_p, d = x_prompt.shape
    n_s, t_s, _ = x_sample.shape
    rows_p, rows_s = n_p * t_p, n_s * t_s
    geo = _Rows(n_p, t_p, n_s, t_s)

    n_c = n_p + n_s
    n_c_pad = -(-n_c // 8) * 8
    c_all = jnp.concatenate([c_sample, c_prompt, jnp.zeros((n_c_pad - n_c, d), f32)], axis=0)
    mod = _modulation(c_all, w_mod, b_mod).reshape(DEPTH, n_c_pad, 1, N_MOD * d)

    p = jax.nn.softmax(hgrn_lb_raw.astype(f32), axis=0)
    lower = jnp.maximum(jnp.cumsum(p, axis=0) - p[0:1], 0.0)

    s0_prompt = jnp.zeros((1, n_p, B_HEADS, B_DK, B_DK), f32)
    states_p = jnp.zeros((DEPTH, n_p, B_HEADS, B_DK, B_DK), f32)
    states_s = jnp.zeros((DEPTH, n_s, B_HEADS, B_DK, B_DK), f32)
    b_up4 = exp_b_up.reshape(DEPTH, N_EXPERTS, 1, 2 * D_FF)
    b_down4 = exp_b_down.reshape(DEPTH, N_EXPERTS, 1, D_MODEL)
    reps = CHUNK // t_s

    x = jnp.concatenate([x_prompt.reshape(rows_p // SEG, SEG, d), x_sample], axis=0)
    moe_out = None
    v_rows = []
    for l in range(DEPTH):
        if l == 0:
            _, h = _norm(geo, x, norm1_g[l], mod=mod, mod_layer=l)
        else:
            x, h = _norm(geo, x, norm1_g[l], y=moe_out, mod=mod, res_layer=l - 1, mod_layer=l)
        z = _in_proj(h, w_in, l)

        ws = gmlp_w_s[l]
        w_mix = jnp.stack([ws, jnp.tile(ws[:, :t_s, :t_s], (1, reps, reps))])
        bs = gmlp_b_s[l]
        b_mix = jnp.stack([jnp.repeat(bs.T, A_GW, axis=1),
                           jnp.repeat(jnp.tile(bs[:, :t_s].T, (reps, 1)), A_GW, axis=1)])
        ya, vn = _gmlp(z, w_mix, b_mix, gmlp_ln_g[l].reshape(1, D_A), gmlp_ln_b[l].reshape(1, D_A), rows_p, t_s)
        v_rows.append(vn[rows_p:].reshape(n_s, t_s, D_A))

        lb = lower[l].reshape(1, D_B)
        vecs = (jnp.log(lb), jnp.log1p(-lb), 1.0 - lb, hgrn_norm_g[l].reshape(1, D_B).astype(f32))
        yb_p, states_p = _hgrn(z, s0_prompt, 0, vecs, n_p, t_p, 0, 1, HGRN_TB, HGRN_C, l, states_p)
        yb_s, states_s = _hgrn(z, state_hgrn, l, vecs, n_s, t_s, rows_p, HGRN_NS, t_s, t_s, l, states_s)
        yb = jnp.concatenate([yb_p, yb_s], axis=0)

        wa, wb, wo = w_branch_a[l].astype(bf16), w_branch_b[l].astype(bf16), w_out[l].astype(bf16)
        x, h2, logits = _merge(geo, ya, yb, z, x, mod, l, wa, wb, wo, norm2_g[l], router_w[l], router_b[l])

        row_tok, row_w, sched, pos = _route(logits)
        xs = h2[row_tok]
        y_rows = _experts(xs, row_w, sched, exp_w_up, b_up4, exp_w_down, b_down4, l)
        moe_out = y_rows[pos.T.reshape(-1)].reshape(TOP_K, geo.rows, d)

    _, y = _norm(geo, x, final_norm_g, y=moe_out, mod=mod, res_layer=DEPTH - 1, out_dtype=f32)
    return (y[:rows_p].reshape(n_p, t_p, d), y[rows_p:].reshape(n_s, t_s, d),
            states_p, states_s, jnp.stack(v_rows))
```

```python
import functools

import jax
import jax.numpy as jnp
from jax import lax
from jax.experimental import pallas as pl
from jax.experimental.pallas import tpu as pltpu

f32 = jnp.float32
bf16 = jnp.bfloat16

D_MODEL = 2048
DEPTH = 4
D_A = D_MODEL // 2
A_GROUPS = 8
A_GW = D_A // A_GROUPS
CHUNK = 128
D_B = D_MODEL // 2
B_DK = 128
B_HEADS = D_B // B_DK
IN_W = 2 * D_A + 4 * D_B + 2 * D_MODEL
N_EXPERTS = 32
TOP_K = 4
D_FF = D_MODEL
SWIGLU_LIMIT = 7.0
SWIGLU_ALPHA = 1.702
N_MOD = 6
EPS = 1e-6
LOG2_E = 1.4426950408889634

VMEM_LIMIT_BYTES = 56 * 1024 * 1024

SEG = 8
ROW_BLOCK = 256
SEGS = ROW_BLOCK // SEG
MERGE_SPLIT = 2
IN_TM = 1024
IN_TN = 1024
MOE_TM = 256
UP_TF = 1024
CAST_K = 512
HGRN_TB = 256
HGRN_C = 16
HGRN_NS = 8


def _params(sem):
    return pltpu.CompilerParams(dimension_semantics=sem, vmem_limit_bytes=VMEM_LIMIT_BYTES)


def _sigmoid(x):
    return jax.nn.sigmoid(x)


def _gelu(x):
    return 0.5 * x * (1.0 + lax.erf(x * 0.7071067811865476))


def _mod_kernel(c_ref, w_ref, b_ref, o_ref):
    c = c_ref[...]
    a = (c * _sigmoid(c)).astype(bf16)
    o_ref[...] = jnp.dot(a, w_ref[...].astype(bf16), preferred_element_type=f32) + b_ref[...]


def _modulation(c_all, w_mod, b_mod):
    n = c_all.shape[0]
    tn = 1024
    return pl.pallas_call(
        _mod_kernel,
        grid=(DEPTH, N_MOD * D_MODEL // tn),
        in_specs=[
            pl.BlockSpec((n, D_MODEL), lambda l, j: (0, 0)),
            pl.BlockSpec((None, D_MODEL, tn), lambda l, j: (l, 0, j)),
            pl.BlockSpec((None, 1, tn), lambda l, j: (l, 0, j)),
        ],
        out_specs=pl.BlockSpec((None, n, tn), lambda l, j: (l, 0, j)),
        out_shape=jax.ShapeDtypeStruct((DEPTH, n, N_MOD * D_MODEL), f32),
        compiler_params=_params(("arbitrary", "arbitrary")),
        name="modulation",
    )(c_all, w_mod, b_mod.reshape(DEPTH, 1, N_MOD * D_MODEL))


class _Rows:
    def __init__(self, n_p, t_p, n_s, t_s):
        assert t_s == SEG and t_p % ROW_BLOCK == 0 and (n_s * t_s) % ROW_BLOCK == 0
        self.n_p, self.n_s = n_p, n_s
        self.blocks_per_seq = t_p // ROW_BLOCK
        self.prompt_blocks = n_p * self.blocks_per_seq
        self.n_blocks = self.prompt_blocks + n_s * t_s // ROW_BLOCK
        self.rows = self.n_blocks * ROW_BLOCK

    def x_spec(self):
        return pl.BlockSpec((SEGS, SEG, D_MODEL), lambda i: (i, 0, 0))

    def row_spec(self, width, colblk=0):
        return pl.BlockSpec((ROW_BLOCK, width), lambda i: (i, colblk))

    def mod_specs(self, layer, col):
        pb, bps, n_p, n_s = self.prompt_blocks, self.blocks_per_seq, self.n_p, self.n_s
        return [pl.BlockSpec((None, 1, 1, D_MODEL), lambda i: (layer, n_s + jnp.minimum(i // bps, n_p - 1), 0, col)),
                pl.BlockSpec((None, SEGS, 1, D_MODEL), lambda i: (layer, jnp.maximum(i - pb, 0), 0, col))]


def _pick_mod(prompt_blocks, p_ref, s_ref):
    return jnp.where(pl.program_id(0) < prompt_blocks, p_ref[...], s_ref[...])


def _norm_kernel(prompt_blocks, has_res, has_mod, *refs):
    refs = list(refs)
    x_ref = refs.pop(0)
    if has_res:
        y_ref, gp_ref, gs_ref = refs.pop(0), refs.pop(0), refs.pop(0)
    g_ref = refs.pop(0)
    if has_mod:
        shp_ref, shs_ref, scp_ref, scs_ref = (refs.pop(0) for _ in range(4))
    if has_res:
        xo_ref = refs.pop(0)
    h_ref = refs.pop(0)

    x = x_ref[...]
    nb, tb, d = x.shape
    if has_res:
        y = y_ref[0]
        for k in range(1, y_ref.shape[0]):
            y = y + y_ref[k]
        x = x + _pick_mod(prompt_blocks, gp_ref, gs_ref) * y.reshape(nb, tb, d)
        xo_ref[...] = x
    hn = x * lax.rsqrt(jnp.mean(x * x, axis=-1, keepdims=True) + EPS) * g_ref[...]
    if has_mod:
        hn = hn * (1.0 + _pick_mod(prompt_blocks, scp_ref, scs_ref)) + _pick_mod(prompt_blocks, shp_ref, shs_ref)
    h_ref[...] = hn.reshape(nb * tb, d).astype(h_ref.dtype)


def _norm(geo, x3, g, *, y=None, mod=None, res_layer=None, mod_layer=None, out_dtype=bf16):
    d = D_MODEL
    has_res, has_mod = y is not None, mod_layer is not None
    ins, specs = [x3], [geo.x_spec()]
    if has_res:
        ins += [y, mod, mod]
        specs += [pl.BlockSpec((y.shape[0], ROW_BLOCK, d), lambda i: (0, i, 0))] + geo.mod_specs(res_layer, 5)
    ins.append(g.reshape(1, 1, d))
    specs.append(pl.BlockSpec((1, 1, d), lambda i: (0, 0, 0)))
    if has_mod:
        ins += [mod] * 4
        specs += geo.mod_specs(mod_layer, 0) + geo.mod_specs(mod_layer, 1)
    out_shapes, out_specs = [], []
    if has_res:
        out_shapes.append(jax.ShapeDtypeStruct(x3.shape, f32))
        out_specs.append(geo.x_spec())
    out_shapes.append(jax.ShapeDtypeStruct((geo.rows, d), out_dtype))
    out_specs.append(geo.row_spec(d))
    res = pl.pallas_call(
        functools.partial(_norm_kernel, geo.prompt_blocks, has_res, has_mod),
        grid=(geo.n_blocks,), in_specs=specs, out_specs=out_specs, out_shape=out_shapes,
        compiler_params=_params(("arbitrary",)),
        name="norm",
    )(*ins)
    return res if has_res else (None, res[0])


def _in_kernel(h_ref, w_ref, o_ref, wb_ref):
    @pl.when(pl.program_id(1) == 0)
    def _():
        wb_ref[...] = w_ref[...].astype(bf16)

    o_ref[...] = jnp.dot(h_ref[...], wb_ref[...], preferred_element_type=f32).astype(bf16)


def _in_proj(h, w_in, layer):
    rows = h.shape[0]
    return pl.pallas_call(
        _in_kernel,
        grid=(IN_W // IN_TN, rows // IN_TM),
        in_specs=[
            pl.BlockSpec((IN_TM, D_MODEL), lambda j, m: (m, 0)),
            pl.BlockSpec((None, D_MODEL, IN_TN), lambda j, m: (layer, 0, j)),
        ],
        out_specs=pl.BlockSpec((IN_TM, IN_TN), lambda j, m: (m, j)),
        out_shape=jax.ShapeDtypeStruct((rows, IN_W), bf16),
        scratch_shapes=[pltpu.VMEM((D_MODEL, IN_TN), bf16)],
        compiler_params=_params(("arbitrary", "arbitrary")),
        name="in_proj",
    )(h, w_in)


def _gmlp_kernel(n_prompt_chunks, sample_shift, u_ref, v_ref, w_ref, b_ref, lg_ref, lb_ref, ya_ref, vn_ref):
    c = pl.program_id(0)
    u = _gelu(u_ref[...].astype(f32))
    v = _gelu(v_ref[...].astype(f32))
    vc = v - jnp.mean(v, axis=-1, keepdims=True)
    vn = vc * lax.rsqrt(jnp.mean(vc * vc, axis=-1, keepdims=True) + EPS) * lg_ref[...] + lb_ref[...]
    vn_ref[...] = vn
    shift = jnp.where(c >= n_prompt_chunks, sample_shift, 7)
    row = lax.broadcasted_iota(jnp.int32, (CHUNK, CHUNK), 0)
    col = lax.broadcasted_iota(jnp.int32, (CHUNK, CHUNK), 1)
    mask = ((row >> shift) == (col >> shift)) & (col <= row)
    vnb = vn.astype(bf16)
    mixed = []
    for g in range(A_GROUPS):
        wg = jnp.where(mask, w_ref[g], 0.0).astype(bf16)
        mixed.append(jnp.dot(wg, vnb[:, g * A_GW:(g + 1) * A_GW], preferred_element_type=f32))
    mixed = jnp.concatenate(mixed, axis=1) + b_ref[...]
    ya_ref[...] = (u * mixed).astype(bf16)


def _gmlp(z, w_mix, b_mix, ln_g, ln_b, rows_p, t_s):
    rows = z.shape[0]
    ncp = rows_p // CHUNK
    return pl.pallas_call(
        functools.partial(_gmlp_kernel, ncp, t_s.bit_length() - 1),
        grid=(rows // CHUNK,),
        in_specs=[
            pl.BlockSpec((CHUNK, D_A), lambda c: (c, 0)),
            pl.BlockSpec((CHUNK, D_A), lambda c: (c, 1)),
            pl.BlockSpec((None, A_GROUPS, CHUNK, CHUNK), lambda c: (jnp.where(c >= ncp, 1, 0), 0, 0, 0)),
            pl.BlockSpec((None, CHUNK, D_A), lambda c: (jnp.where(c >= ncp, 1, 0), 0, 0)),
            pl.BlockSpec((1, D_A), lambda c: (0, 0)),
            pl.BlockSpec((1, D_A), lambda c: (0, 0)),
        ],
        out_specs=[pl.BlockSpec((CHUNK, D_A), lambda c: (c, 0)), pl.BlockSpec((CHUNK, D_A), lambda c: (c, 0))],
        out_shape=[jax.ShapeDtypeStruct((rows, D_A), bf16), jax.ShapeDtypeStruct((rows, D_A), f32)],
        compiler_params=_params(("arbitrary",)),
        name="gmlp",
    )(z, z, w_mix, b_mix, ln_g, ln_b)


def _hgrn_kernel(ns, tb, c, q_ref, f_ref, i_ref, g_ref, s0_ref, llb_ref, l1m_ref, oml_ref, hng_ref, all_ref,
                 yb_ref, s_ref):
    del all_ref

    @pl.when(pl.program_id(1) == 0)
    def _():
        s_ref[...] = s0_ref[...]

    row = lax.broadcasted_iota(jnp.int32, (c, c), 0)
    col = lax.broadcasted_iota(jnp.int32, (c, c), 1)
    tri = (col <= row).astype(f32)
    n_tiles = c // SEG
    rows1 = lax.broadcasted_iota(jnp.int32, (SEG, 1), 0)
    ones = jnp.ones((B_DK, B_DK), bf16)
    llb, l1m, oml, hng = llb_ref[...], l1m_ref[...], oml_ref[...], hng_ref[...]

    def chunk(si, r0):
        zq = q_ref[pl.ds(r0, c), :].astype(f32)
        zf = f_ref[pl.ds(r0, c), :].astype(f32)
        zi = i_ref[pl.ds(r0, c), :].astype(f32)
        zg = g_ref[pl.ds(r0, c), :].astype(f32)
        qs = zq * _sigmoid(zq)
        a = l1m + jnp.minimum(zf, 0.0) - jnp.log1p(jnp.exp(-jnp.abs(zf)))
        logf = jnp.maximum(a, llb) + jnp.log1p(jnp.exp(-jnp.abs(a - llb)))
        kk = oml * _sigmoid(-zf)
        gc = jnp.dot(tri, logf, preferred_element_type=f32, precision=lax.Precision.HIGHEST)
        gc2 = gc * LOG2_E
        g_last = gc[c - 1:c, :]
        q_dec = (qs * jnp.exp(gc)).astype(bf16)
        k_dec = (kk * jnp.exp(g_last - gc)).astype(bf16)
        vb = zi.astype(bf16)
        dec = jnp.exp(jnp.concatenate([g_last[:, h * B_DK:(h + 1) * B_DK] for h in range(B_HEADS)], axis=0))
        dec_t = dec.T
        gate = zg * _sigmoid(zg)
        for h in range(B_HEADS):
            sl = slice(h * B_DK, (h + 1) * B_DK)
            s_prev = s_ref[si, h]
            o = jnp.dot(q_dec[:, sl], s_prev.astype(bf16), preferred_element_type=f32)
            qh, gh, kh, vh = qs[:, sl], gc2[:, sl], kk[:, sl], zi[:, sl]
            q_tiles = [qh[SEG * r:SEG * (r + 1), :] for r in range(n_tiles)]
            g_tiles = [gh[SEG * r:SEG * (r + 1), :] for r in range(n_tiles)]
            terms = []
            for s in range(c):
                gs, ks = gh[s:s + 1, :], kh[s:s + 1, :]
                for r in range(s // SEG, n_tiles):
                    e = jnp.exp2(g_tiles[r] - gs)
                    if r == s // SEG:
                        e = jnp.where(rows1 >= s % SEG, e, 0.0)
                    terms.append(q_tiles[r] * (ks * e))
            a_rows = jnp.dot(jnp.concatenate(terms, axis=0).astype(bf16), ones, preferred_element_type=f32)
            o_tiles = [o[SEG * r:SEG * (r + 1), :] for r in range(n_tiles)]
            n = 0
            for s in range(c):
                for r in range(s // SEG, n_tiles):
                    o_tiles[r] = o_tiles[r] + a_rows[SEG * n:SEG * (n + 1), :] * vh[s:s + 1, :]
                    n += 1
            o = jnp.concatenate(o_tiles, axis=0)
            upd = lax.dot_general(k_dec[:, sl], vb[:, sl], (((0,), (0,)), ((), ())),
                                  preferred_element_type=f32)
            s_ref[si, h] = dec_t[:, h:h + 1] * s_prev + upd
            on = o * lax.rsqrt(jnp.mean(o * o, axis=-1, keepdims=True) + EPS) * hng[:, sl]
            yb_ref[pl.ds(r0, c), sl] = (on * gate[:, sl]).astype(bf16)

    n_chunks = tb // c
    for si in range(ns):
        if n_chunks == 1:
            chunk(si, si * tb)
        else:
            def body(ci, carry, si=si):
                chunk(si, pl.multiple_of(si * tb + ci * c, c))
                return carry
            lax.fori_loop(0, n_chunks, body, 0, unroll=2)


def _hgrn(z, s0, s0_layer, vecs, n_seq, t, row0, ns, tb, c, layer, states):
    rows = ns * tb
    tpb = t // tb
    base = row0 // rows
    grid = (n_seq // ns, tpb)

    def zspec(colblk):
        return pl.BlockSpec((rows, D_B), lambda n, j: (base + n * tpb + j, colblk))

    vspec = pl.BlockSpec((1, D_B), lambda n, j: (0, 0))
    ins = [z, z, z, z, s0, *vecs, states]
    specs = [zspec(2), zspec(3), zspec(4), zspec(5),
             pl.BlockSpec((None, ns, B_HEADS, B_DK, B_DK), lambda n, j: (s0_layer, n, 0, 0, 0)),
             vspec, vspec, vspec, vspec, pl.BlockSpec(memory_space=pl.ANY)]
    yb, states = pl.pallas_call(
        functools.partial(_hgrn_kernel, ns, tb, c),
        grid=grid,
        in_specs=specs,
        out_specs=[pl.BlockSpec((rows, D_B), lambda n, j: (n * tpb + j, 0)),
                   pl.BlockSpec((None, ns, B_HEADS, B_DK, B_DK), lambda n, j: (layer, n, 0, 0, 0))],
        out_shape=[jax.ShapeDtypeStruct((n_seq * t, D_B), bf16),
                   jax.ShapeDtypeStruct(states.shape, f32)],
        input_output_aliases={len(ins) - 1: 1},
        compiler_params=_params(("arbitrary", "arbitrary")),
        name="hgrn",
    )(*ins)
    return yb, states


def _merge_kernel(prompt_blocks, ya_ref, yb_ref, ga_ref, gb_ref, x_ref, g1p_ref, g1s_ref, wa_ref, wb_ref, wo_ref,
                  n2_ref, shp_ref, shs_ref, scp_ref, scs_ref, rw_ref, rb_ref, xo_ref, h2_ref, lg_ref):
    gate1 = _pick_mod(prompt_blocks, g1p_ref, g1s_ref)
    scale2 = _pick_mod(prompt_blocks, scp_ref, scs_ref)
    shift2 = _pick_mod(prompt_blocks, shp_ref, shs_ref)
    d = D_MODEL
    segs = SEGS // MERGE_SPLIT
    rows = segs * SEG
    for part in range(MERGE_SPLIT):
        r = slice(part * rows, (part + 1) * rows)
        sg = slice(part * segs, (part + 1) * segs)
        a = jnp.dot(ya_ref[r, :], wa_ref[...], preferred_element_type=f32)
        b = jnp.dot(yb_ref[r, :], wb_ref[...], preferred_element_type=f32)
        merged = _sigmoid(ga_ref[r, :].astype(f32)) * a + _sigmoid(gb_ref[r, :].astype(f32)) * b
        mix = jnp.dot(merged.astype(bf16), wo_ref[...], preferred_element_type=f32)
        x = x_ref[sg] + gate1[sg] * mix.reshape(segs, SEG, d)
        xo_ref[sg] = x
        hn = x * lax.rsqrt(jnp.mean(x * x, axis=-1, keepdims=True) + EPS) * n2_ref[...]
        hn = (hn * (1.0 + scale2[sg]) + shift2[sg]).reshape(rows, d)
        h2_ref[r, :] = hn
        hi = hn.astype(bf16)
        lo = (hn - hi.astype(f32)).astype(bf16)
        hh_hl = jnp.dot(hi, rw_ref[...], preferred_element_type=f32)
        lh = jnp.dot(lo, rw_ref[:, :N_EXPERTS], preferred_element_type=f32)
        lg_ref[r, :] = hh_hl[:, :N_EXPERTS] + hh_hl[:, N_EXPERTS:] + lh + rb_ref[...]


def _merge(geo, ya, yb, z, x3, mod, layer, wa, wb, wo, n2g, rw, rb):
    d = D_MODEL
    ga_blk = (2 * D_A + 4 * D_B) // d

    def const(shape):
        return pl.BlockSpec(shape, lambda i: tuple(0 for _ in shape), pipeline_mode=pl.Buffered(1))

    rw_hi = rw.astype(bf16)
    rw_split = jnp.concatenate([rw_hi, (rw - rw_hi.astype(f32)).astype(bf16)], axis=1)
    ins = [ya, yb, z, z, x3, mod, mod, wa, wb, wo, n2g.reshape(1, 1, d), mod, mod, mod, mod,
           rw_split, rb.reshape(1, N_EXPERTS)]
    specs = ([geo.row_spec(D_A), geo.row_spec(D_B), geo.row_spec(d, ga_blk), geo.row_spec(d, ga_blk + 1), geo.x_spec()]
             + geo.mod_specs(layer, 2)
             + [const((D_A, d)), const((D_B, d)), const((d, d)), const((1, 1, d))]
             + geo.mod_specs(layer, 3) + geo.mod_specs(layer, 4)
             + [const((d, 2 * N_EXPERTS)), const((1, N_EXPERTS))])
    return pl.pallas_call(
        functools.partial(_merge_kernel, geo.prompt_blocks),
        grid=(geo.n_blocks,),
        in_specs=specs,
        out_specs=[geo.x_spec(), geo.row_spec(d), geo.row_spec(N_EXPERTS)],
        out_shape=[jax.ShapeDtypeStruct(x3.shape, f32), jax.ShapeDtypeStruct((geo.rows, d), f32),
                   jax.ShapeDtypeStruct((geo.rows, N_EXPERTS), f32)],
        compiler_params=_params(("arbitrary",)),
        name="merge",
    )(*ins)


def _up_kernel(layer, nf, be_ref, first_ref, tix_ref, nxt_ref, meta_ref, xs_ref, w_hbm, bg_ref, bl_ref, o_ref,
               stage, wgb_ref, wlb_ref, sem):
    j, i = pl.program_id(0), pl.program_id(1)
    n_tiles = meta_ref[1]

    def tile_copies(e, jj, slot):
        return [pltpu.make_async_copy(
            w_hbm.at[layer, e, :, pl.ds(pl.multiple_of((half * nf + jj) * UP_TF, UP_TF), UP_TF)],
            stage.at[slot, half], sem.at[slot, half]) for half in range(2)]

    @pl.when(first_ref[i] == 1)
    def _():
        q = j * n_tiles + tix_ref[i]
        slot = q % 2

        @pl.when(q == 0)
        def _():
            for cp in tile_copies(be_ref[i], j, slot):
                cp.start()

        for cp in tile_copies(be_ref[i], j, slot):
            cp.wait()
        next_j = jnp.where(tix_ref[i] == n_tiles - 1, j + 1, j)

        @pl.when(next_j < nf)
        def _():
            for cp in tile_copies(nxt_ref[i], next_j, 1 - slot):
                cp.start()

    def finish(gate, lin):
        gate = jnp.minimum(gate, SWIGLU_LIMIT)
        lin = jnp.clip(lin, -SWIGLU_LIMIT, SWIGLU_LIMIT)
        glu = gate * _sigmoid(SWIGLU_ALPHA * gate)
        o_ref[...] = ((lin + 1.0) * glu).astype(bf16)

    @pl.when(first_ref[i] == 1)
    def _():
        slot = (j * n_tiles + tix_ref[i]) % 2
        x = xs_ref[...].astype(bf16)
        gate, lin = bg_ref[...], bl_ref[...]
        for kc in range(D_MODEL // CAST_K):
            ks = slice(kc * CAST_K, (kc + 1) * CAST_K)
            wg = stage[slot, 0, ks, :].astype(bf16)
            wl = stage[slot, 1, ks, :].astype(bf16)
            wgb_ref[ks, :] = wg
            wlb_ref[ks, :] = wl
            gate = gate + jnp.dot(x[:, ks], wg, preferred_element_type=f32)
            lin = lin + jnp.dot(x[:, ks], wl, preferred_element_type=f32)
        finish(gate, lin)

    @pl.when((first_ref[i] == 0) & (i < meta_ref[0]))
    def _():
        x = xs_ref[...].astype(bf16)
        finish(jnp.dot(x, wgb_ref[...], preferred_element_type=f32) + bg_ref[...],
               jnp.dot(x, wlb_ref[...], preferred_element_type=f32) + bl_ref[...])

    @pl.when(i >= meta_ref[0])
    def _():
        o_ref[...] = jnp.zeros_like(o_ref)


def _down_kernel(layer, be_ref, first_ref, tix_ref, nxt_ref, meta_ref, a_ref, w_hbm, b_ref, rw_ref, o_ref,
                 stage, wb_ref, sem):
    i = pl.program_id(0)

    def tile_copy(e, slot):
        return pltpu.make_async_copy(w_hbm.at[layer, e], stage.at[slot], sem.at[slot])

    @pl.when(first_ref[i] == 1)
    def _():
        q = tix_ref[i]
        slot = q % 2

        @pl.when(q == 0)
        def _():
            tile_copy(be_ref[i], slot).start()

        tile_copy(be_ref[i], slot).wait()

        @pl.when(q + 1 < meta_ref[1])
        def _():
            tile_copy(nxt_ref[i], 1 - slot).start()

    @pl.when(first_ref[i] == 1)
    def _():
        slot = tix_ref[i] % 2
        y = b_ref[...]
        for kc in range(D_FF // CAST_K):
            ks = slice(kc * CAST_K, (kc + 1) * CAST_K)
            w = stage[slot, ks, :].astype(bf16)
            wb_ref[ks, :] = w
            y = y + jnp.dot(a_ref[:, ks], w, preferred_element_type=f32)
        o_ref[...] = y * rw_ref[...]

    @pl.when((first_ref[i] == 0) & (i < meta_ref[0]))
    def _():
        y = jnp.dot(a_ref[...], wb_ref[...], preferred_element_type=f32) + b_ref[...]
        o_ref[...] = y * rw_ref[...]

    @pl.when(i >= meta_ref[0])
    def _():
        o_ref[...] = jnp.zeros_like(o_ref)


def _experts(xs, row_w, sched, w_up, b_up, w_down, b_down, layer):
    n_rows = xs.shape[0]
    nb = n_rows // MOE_TM
    nf = D_FF // UP_TF
    act = pl.pallas_call(
        functools.partial(_up_kernel, layer, nf),
        grid_spec=pltpu.PrefetchScalarGridSpec(
            num_scalar_prefetch=5,
            grid=(nf, nb),
            in_specs=[
                pl.BlockSpec((MOE_TM, D_MODEL), lambda j, i, be, *_: (i, 0)),
                pl.BlockSpec(memory_space=pl.ANY),
                pl.BlockSpec((None, None, 1, UP_TF), lambda j, i, be, *_: (layer, be[i], 0, j)),
                pl.BlockSpec((None, None, 1, UP_TF), lambda j, i, be, *_: (layer, be[i], 0, nf + j)),
            ],
            out_specs=pl.BlockSpec((MOE_TM, UP_TF), lambda j, i, be, *_: (i, j)),
            scratch_shapes=[pltpu.VMEM((2, 2, D_MODEL, UP_TF), f32),
                            pltpu.VMEM((D_MODEL, UP_TF), bf16), pltpu.VMEM((D_MODEL, UP_TF), bf16),
                            pltpu.SemaphoreType.DMA((2, 2))],
        ),
        out_shape=jax.ShapeDtypeStruct((n_rows, D_FF), bf16),
        compiler_params=_params(("arbitrary", "arbitrary")),
        name="moe_up",
    )(*sched, xs, w_up, b_up, b_up)
    return pl.pallas_call(
        functools.partial(_down_kernel, layer),
        grid_spec=pltpu.PrefetchScalarGridSpec(
            num_scalar_prefetch=5,
            grid=(nb,),
            in_specs=[
                pl.BlockSpec((MOE_TM, D_FF), lambda i, be, *_: (i, 0)),
                pl.BlockSpec(memory_space=pl.ANY),
                pl.BlockSpec((None, None, 1, D_MODEL), lambda i, be, *_: (layer, be[i], 0, 0)),
                pl.BlockSpec((MOE_TM, 1), lambda i, be, *_: (i, 0)),
            ],
            out_specs=pl.BlockSpec((MOE_TM, D_MODEL), lambda i, be, *_: (i, 0)),
            scratch_shapes=[pltpu.VMEM((2, D_FF, D_MODEL), f32), pltpu.VMEM((D_FF, D_MODEL), bf16),
                            pltpu.SemaphoreType.DMA((2,))],
        ),
        out_shape=jax.ShapeDtypeStruct((n_rows, D_MODEL), f32),
        compiler_params=_params(("arbitrary",)),
        name="moe_down",
    )(*sched, act, w_down, b_down, row_w)


def _route(logits):
    t = logits.shape[0]
    tk = t * TOP_K
    lane = jnp.arange(N_EXPERTS, dtype=jnp.int32)[None, :]
    rest, idx, val = logits, [], []
    for _ in range(TOP_K):
        i = jnp.argmax(rest, axis=1).astype(jnp.int32)
        idx.append(i)
        val.append(jnp.max(rest, axis=1))
        rest = jnp.where(lane == i[:, None], -jnp.inf, rest)
    top_e = jnp.stack(idx, axis=1)
    weights = jax.nn.softmax(jnp.stack(val, axis=1), axis=-1)
    member = sum((lane == i[:, None]).astype(jnp.int32) for i in idx)
    csum = jnp.cumsum(member, axis=0)
    counts = csum[-1]
    rank = csum - member
    padded = (counts + MOE_TM - 1) // MOE_TM * MOE_TM
    pad_end = jnp.cumsum(padded)
    pad_start = pad_end - padded
    start = jnp.cumsum(counts) - counts
    slot_row = pad_start[None, :] + rank
    pos = jnp.stack([jnp.sum(jnp.where(lane == i[:, None], slot_row, 0), axis=1) for i in idx], axis=1)
    n_blocks = tk // MOE_TM + N_EXPERTS
    n_rows = n_blocks * MOE_TM
    blk = jnp.arange(n_blocks, dtype=jnp.int32)
    blk_start = blk * MOE_TM
    block_e = jnp.minimum(jnp.sum(pad_end[None, :] <= blk_start[:, None], axis=1), N_EXPERTS - 1).astype(jnp.int32)
    n_valid = (pad_end[-1] // MOE_TM).astype(jnp.int32)
    first = ((blk_start == pad_start[block_e]) & (blk_start < pad_end[-1])).astype(jnp.int32)
    tix = jnp.maximum(jnp.cumsum(first) - 1, 0).astype(jnp.int32)
    first_pos = jnp.where(first == 1, blk, n_blocks)
    next_pos = jnp.concatenate([lax.cummin(first_pos, reverse=True)[1:], jnp.full((1,), n_blocks, jnp.int32)])
    nxt = jnp.where(next_pos < n_blocks, block_e[jnp.minimum(next_pos, n_blocks - 1)], block_e[0]).astype(jnp.int32)
    meta = jnp.stack([n_valid, jnp.sum(first).astype(jnp.int32)])
    order = jnp.argsort(top_e.reshape(tk), stable=True).astype(jnp.int32)
    in_blk = jnp.arange(MOE_TM, dtype=jnp.int32)[None, :]
    offset = (blk_start - pad_start[block_e])[:, None] + in_blk
    live = (offset < counts[block_e][:, None]).reshape(n_rows)
    slot = order[jnp.clip(start[block_e][:, None] + offset, 0, tk - 1).reshape(n_rows)]
    row_tok = jnp.where(live, slot // TOP_K, 0)
    row_w = jnp.where(live, weights.reshape(tk)[slot], 0.0)
    return row_tok, row_w.reshape(n_rows, 1), (block_e, first, tix, nxt, meta), pos.astype(jnp.int32)


def kernel(x_prompt, x_sample, c_prompt, c_sample, state_hgrn, w_mod, b_mod, norm1_g, w_in, gmlp_ln_g, gmlp_ln_b, gmlp_w_s, gmlp_b_s, hgrn_lb_raw, hgrn_norm_g, w_branch_a, w_branch_b, w_out, norm2_g, router_w, router_b, exp_w_up, exp_b_up, exp_w_down, exp_b_down, final_norm_g):
    n_p, t_p, d = x_prompt.shape
    n_s, t_s, _ = x_sample.shape
    rows_p, rows_s = n_p * t_p, n_s * t_s
    geo = _Rows(n_p, t_p, n_s, t_s)

    n_c = n_p + n_s
    n_c_pad = -(-n_c // 8) * 8
    c_all = jnp.concatenate([c_sample, c_prompt, jnp.zeros((n_c_pad - n_c, d), f32)], axis=0)
    mod = _modulation(c_all, w_mod, b_mod).reshape(DEPTH, n_c_pad, 1, N_MOD * d)

    p = jax.nn.softmax(hgrn_lb_raw.astype(f32), axis=0)
    lower = jnp.maximum(jnp.cumsum(p, axis=0) - p[0:1], 0.0)

    s0_prompt = jnp.zeros((1, n_p, B_HEADS, B_DK, B_DK), f32)
    states_p = jnp.zeros((DEPTH, n_p, B_HEADS, B_DK, B_DK), f32)
    states_s = jnp.zeros((DEPTH, n_s, B_HEADS, B_DK, B_DK), f32)
    b_up4 = exp_b_up.reshape(DEPTH, N_EXPERTS, 1, 2 * D_FF)
    b_down4 = exp_b_down.reshape(DEPTH, N_EXPERTS, 1, D_MODEL)
    reps = CHUNK // t_s

    x = jnp.concatenate([x_prompt.reshape(rows_p // SEG, SEG, d), x_sample], axis=0)
    moe_out = None
    v_rows = []
    for l in range(DEPTH):
        if l == 0:
            _, h = _norm(geo, x, norm1_g[l], mod=mod, mod_layer=l)
        else:
            x, h = _norm(geo, x, norm1_g[l], y=moe_out, mod=mod, res_layer=l - 1, mod_layer=l)
        z = _in_proj(h, w_in, l)

        ws = gmlp_w_s[l]
        w_mix = jnp.stack([ws, jnp.tile(ws[:, :t_s, :t_s], (1, reps, reps))])
        bs = gmlp_b_s[l]
        b_mix = jnp.stack([jnp.repeat(bs.T, A_GW, axis=1),
                           jnp.repeat(jnp.tile(bs[:, :t_s].T, (reps, 1)), A_GW, axis=1)])
        ya, vn = _gmlp(z, w_mix, b_mix, gmlp_ln_g[l].reshape(1, D_A), gmlp_ln_b[l].reshape(1, D_A), rows_p, t_s)
        v_rows.append(vn[rows_p:].reshape(n_s, t_s, D_A))

        lb = lower[l].reshape(1, D_B)
        vecs = (jnp.log(lb), jnp.log1p(-lb), 1.0 - lb, hgrn_norm_g[l].reshape(1, D_B).astype(f32))
        yb_p, states_p = _hgrn(z, s0_prompt, 0, vecs, n_p, t_p, 0, 1, HGRN_TB, HGRN_C, l, states_p)
        yb_s, states_s = _hgrn(z, state_hgrn, l, vecs, n_s, t_s, rows_p, HGRN_NS, t_s, t_s, l, states_s)
        yb = jnp.concatenate([yb_p, yb_s], axis=0)

        wa, wb, wo = w_branch_a[l].astype(bf16), w_branch_b[l].astype(bf16), w_out[l].astype(bf16)
        x, h2, logits = _merge(geo, ya, yb, z, x, mod, l, wa, wb, wo, norm2_g[l], router_w[l], router_b[l])

        row_tok, row_w, sched, pos = _route(logits)
        xs = h2[row_tok]
        y_rows = _experts(xs, row_w, sched, exp_w_up, b_up4, exp_w_down, b_down4, l)
        moe_out = y_rows[pos.T.reshape(-1)].reshape(TOP_K, geo.rows, d)

    _, y = _norm(geo, x, final_norm_g, y=moe_out, mod=mod, res_layer=DEPTH - 1, out_dtype=f32)
    return (y[:rows_p].reshape(n_p, t_p, d), y[rows_p:].reshape(n_s, t_s, d),
            states_p, states_s, jnp.stack(v_rows))
```

```python
import functools

import jax
import jax.numpy as jnp
from jax import lax
from jax.experimental import pallas as pl
from jax.experimental.pallas import tpu as pltpu

f32 = jnp.float32
bf16 = jnp.bfloat16

D_MODEL = 2048
DEPTH = 4
D_A = D_MODEL // 2
A_GROUPS = 8
A_GW = D_A // A_GROUPS
CHUNK = 128
D_B = D_MODEL // 2
B_DK = 128
B_HEADS = D_B // B_DK
IN_W = 2 * D_A + 4 * D_B + 2 * D_MODEL
N_EXPERTS = 32
TOP_K = 4
D_FF = D_MODEL
SWIGLU_LIMIT = 7.0
SWIGLU_ALPHA = 1.702
N_MOD = 6
EPS = 1e-6
LOG2_E = 1.4426950408889634

VMEM_LIMIT_BYTES = 56 * 1024 * 1024

SEG = 8
ROW_BLOCK = 256
SEGS = ROW_BLOCK // SEG
MERGE_SPLIT = 2
IN_TM = 1024
IN_TN = 1024
MOE_TM = 256
UP_TF = 1024
CAST_K = 512
HGRN_TB = 256
HGRN_C = 16
HGRN_NS = 8


def _params(sem):
    return pltpu.CompilerParams(dimension_semantics=sem, vmem_limit_bytes=VMEM_LIMIT_BYTES)


def _sigmoid(x):
    return jax.nn.sigmoid(x)


def _gelu(x):
    return 0.5 * x * (1.0 + lax.erf(x * 0.7071067811865476))


def _mod_kernel(c_ref, w_ref, b_ref, o_ref):
    c = c_ref[...]
    a = (c * _sigmoid(c)).astype(bf16)
    o_ref[...] = jnp.dot(a, w_ref[...].astype(bf16), preferred_element_type=f32) + b_ref[...]


def _modulation(c_all, w_mod, b_mod):
    n = c_all.shape[0]
    tn = 1024
    return pl.pallas_call(
        _mod_kernel,
        grid=(DEPTH, N_MOD * D_MODEL // tn),
        in_specs=[
            pl.BlockSpec((n, D_MODEL), lambda l, j: (0, 0)),
            pl.BlockSpec((None, D_MODEL, tn), lambda l, j: (l, 0, j)),
            pl.BlockSpec((None, 1, tn), lambda l, j: (l, 0, j)),
        ],
        out_specs=pl.BlockSpec((None, n, tn), lambda l, j: (l, 0, j)),
        out_shape=jax.ShapeDtypeStruct((DEPTH, n, N_MOD * D_MODEL), f32),
        compiler_params=_params(("arbitrary", "arbitrary")),
        name="modulation",
    )(c_all, w_mod, b_mod.reshape(DEPTH, 1, N_MOD * D_MODEL))


class _Rows:
    def __init__(self, n_p, t_p, n_s, t_s):
        assert t_s == SEG and t_p % ROW_BLOCK == 0 and (n_s * t_s) % ROW_BLOCK == 0
        self.n_p, self.n_s = n_p, n_s
        self.blocks_per_seq = t_p // ROW_BLOCK
        self.prompt_blocks = n_p * self.blocks_per_seq
        self.n_blocks = self.prompt_blocks + n_s * t_s // ROW_BLOCK
        self.rows = self.n_blocks * ROW_BLOCK

    def x_spec(self):
        return pl.BlockSpec((SEGS, SEG, D_MODEL), lambda i: (i, 0, 0))

    def row_spec(self, width, colblk=0):
        return pl.BlockSpec((ROW_BLOCK, width), lambda i: (i, colblk))

    def mod_specs(self, layer, col):
        pb, bps, n_p, n_s = self.prompt_blocks, self.blocks_per_seq, self.n_p, self.n_s
        return [pl.BlockSpec((None, 1, 1, D_MODEL), lambda i: (layer, n_s + jnp.minimum(i // bps, n_p - 1), 0, col)),
                pl.BlockSpec((None, SEGS, 1, D_MODEL), lambda i: (layer, jnp.maximum(i - pb, 0), 0, col))]


def _pick_mod(prompt_blocks, p_ref, s_ref):
    return jnp.where(pl.program_id(0) < prompt_blocks, p_ref[...], s_ref[...])


def _norm_kernel(prompt_blocks, has_res, has_mod, *refs):
    refs = list(refs)
    x_ref = refs.pop(0)
    if has_res:
        y_ref, gp_ref, gs_ref = refs.pop(0), refs.pop(0), refs.pop(0)
    g_ref = refs.pop(0)
    if has_mod:
        shp_ref, shs_ref, scp_ref, scs_ref = (refs.pop(0) for _ in range(4))
    if has_res:
        xo_ref = refs.pop(0)
    h_ref = refs.pop(0)

    x = x_ref[...]
    nb, tb, d = x.shape
    if has_res:
        y = y_ref[0]
        for k in range(1, y_ref.shape[0]):
            y = y + y_ref[k]
        x = x + _pick_mod(prompt_blocks, gp_ref, gs_ref) * y.reshape(nb, tb, d)
        xo_ref[...] = x
    hn = x * lax.rsqrt(jnp.mean(x * x, axis=-1, keepdims=True) + EPS) * g_ref[...]
    if has_mod:
        hn = hn * (1.0 + _pick_mod(prompt_blocks, scp_ref, scs_ref)) + _pick_mod(prompt_blocks, shp_ref, shs_ref)
    h_ref[...] = hn.reshape(nb * tb, d).astype(h_ref.dtype)


def _norm(geo, x3, g, *, y=None, mod=None, res_layer=None, mod_layer=None, out_dtype=bf16):
    d = D_MODEL
    has_res, has_mod = y is not None, mod_layer is not None
    ins, specs = [x3], [geo.x_spec()]
    if has_res:
        ins += [y, mod, mod]
        specs += [pl.BlockSpec((y.shape[0], ROW_BLOCK, d), lambda i: (0, i, 0))] + geo.mod_specs(res_layer, 5)
    ins.append(g.reshape(1, 1, d))
    specs.append(pl.BlockSpec((1, 1, d), lambda i: (0, 0, 0)))
    if has_mod:
        ins += [mod] * 4
        specs += geo.mod_specs(mod_layer, 0) + geo.mod_specs(mod_layer, 1)
    out_shapes, out_specs = [], []
    if has_res:
        out_shapes.append(jax.ShapeDtypeStruct(x3.shape, f32))
        out_specs.append(geo.x_spec())
    out_shapes.append(jax.ShapeDtypeStruct((geo.rows, d), out_dtype))
    out_specs.append(geo.row_spec(d))
    res = pl.pallas_call(
        functools.partial(_norm_kernel, geo.prompt_blocks, has_res, has_mod),
        grid=(geo.n_blocks,), in_specs=specs, out_specs=out_specs, out_shape=out_shapes,
        compiler_params=_params(("arbitrary",)),
        name="norm",
    )(*ins)
    return res if has_res else (None, res[0])


def _in_kernel(h_ref, w_ref, o_ref, wb_ref):
    @pl.when(pl.program_id(1) == 0)
    def _():
        wb_ref[...] = w_ref[...].astype(bf16)

    o_ref[...] = jnp.dot(h_ref[...], wb_ref[...], preferred_element_type=f32).astype(bf16)


def _in_proj(h, w_in, layer):
    rows = h.shape[0]
    return pl.pallas_call(
        _in_kernel,
        grid=(IN_W // IN_TN, rows // IN_TM),
        in_specs=[
            pl.BlockSpec((IN_TM, D_MODEL), lambda j, m: (m, 0)),
            pl.BlockSpec((None, D_MODEL, IN_TN), lambda j, m: (layer, 0, j)),
        ],
        out_specs=pl.BlockSpec((IN_TM, IN_TN), lambda j, m: (m, j)),
        out_shape=jax.ShapeDtypeStruct((rows, IN_W), bf16),
        scratch_shapes=[pltpu.VMEM((D_MODEL, IN_TN), bf16)],
        compiler_params=_params(("arbitrary", "arbitrary")),
        name="in_proj",
    )(h, w_in)


def _gmlp_kernel(n_prompt_chunks, sample_shift, u_ref, v_ref, w_ref, b_ref, lg_ref, lb_ref, ya_ref, vn_ref):
    c = pl.program_id(0)
    u = _gelu(u_ref[...].astype(f32))
    v = _gelu(v_ref[...].astype(f32))
    vc = v - jnp.mean(v, axis=-1, keepdims=True)
    vn = vc * lax.rsqrt(jnp.mean(vc * vc, axis=-1, keepdims=True) + EPS) * lg_ref[...] + lb_ref[...]
    vn_ref[...] = vn
    shift = jnp.where(c >= n_prompt_chunks, sample_shift, 7)
    row = lax.broadcasted_iota(jnp.int32, (CHUNK, CHUNK), 0)
    col = lax.broadcasted_iota(jnp.int32, (CHUNK, CHUNK), 1)
    mask = ((row >> shift) == (col >> shift)) & (col <= row)
    vnb = vn.astype(bf16)
    mixed = []
    for g in range(A_GROUPS):
        wg = jnp.where(mask, w_ref[g], 0.0).astype(bf16)
        mixed.append(jnp.dot(wg, vnb[:, g * A_GW:(g + 1) * A_GW], preferred_element_type=f32))
    mixed = jnp.concatenate(mixed, axis=1) + b_ref[...]
    ya_ref[...] = (u * mixed).astype(bf16)


def _gmlp(z, w_mix, b_mix, ln_g, ln_b, rows_p, t_s):
    rows = z.shape[0]
    ncp = rows_p // CHUNK
    return pl.pallas_call(
        functools.partial(_gmlp_kernel, ncp, t_s.bit_length() - 1),
        grid=(rows // CHUNK,),
        in_specs=[
            pl.BlockSpec((CHUNK, D_A), lambda c: (c, 0)),
            pl.BlockSpec((CHUNK, D_A), lambda c: (c, 1)),
            pl.BlockSpec((None, A_GROUPS, CHUNK, CHUNK), lambda c: (jnp.where(c >= ncp, 1, 0), 0, 0, 0)),
            pl.BlockSpec((None, CHUNK, D_A), lambda c: (jnp.where(c >= ncp, 1, 0), 0, 0)),
            pl.BlockSpec((1, D_A), lambda c: (0, 0)),
            pl.BlockSpec((1, D_A), lambda c: (0, 0)),
        ],
        out_specs=[pl.BlockSpec((CHUNK, D_A), lambda c: (c, 0)), pl.BlockSpec((CHUNK, D_A), lambda c: (c, 0))],
        out_shape=[jax.ShapeDtypeStruct((rows, D_A), bf16), jax.ShapeDtypeStruct((rows, D_A), f32)],
        compiler_params=_params(("arbitrary",)),
        name="gmlp",
    )(z, z, w_mix, b_mix, ln_g, ln_b)


def _hgrn_kernel(ns, tb, c, q_ref, f_ref, i_ref, g_ref, s0_ref, llb_ref, l1m_ref, oml_ref, hng_ref, all_ref,
                 yb_ref, s_ref):
    del all_ref

    @pl.when(pl.program_id(1) == 0)
    def _():
        s_ref[...] = s0_ref[...]

    row = lax.broadcasted_iota(jnp.int32, (c, c), 0)
    col = lax.broadcasted_iota(jnp.int32, (c, c), 1)
    tri = (col <= row).astype(f32)
    n_tiles = c // SEG
    rows1 = lax.broadcasted_iota(jnp.int32, (SEG, 1), 0)
    ones = jnp.ones((B_DK, B_DK), bf16)
    llb, l1m, oml, hng = llb_ref[...], l1m_ref[...], oml_ref[...], hng_ref[...]

    def chunk(si, r0):
        zq = q_ref[pl.ds(r0, c), :].astype(f32)
        zf = f_ref[pl.ds(r0, c), :].astype(f32)
        zi = i_ref[pl.ds(r0, c), :].astype(f32)
        zg = g_ref[pl.ds(r0, c), :].astype(f32)
        qs = zq * _sigmoid(zq)
        a = l1m + jnp.minimum(zf, 0.0) - jnp.log1p(jnp.exp(-jnp.abs(zf)))
        logf = jnp.maximum(a, llb) + jnp.log1p(jnp.exp(-jnp.abs(a - llb)))
        kk = oml * _sigmoid(-zf)
        gc = jnp.dot(tri, logf, preferred_element_type=f32, precision=lax.Precision.HIGHEST)
        gc2 = gc * LOG2_E
        g_last = gc[c - 1:c, :]
        q_dec = (qs * jnp.exp(gc)).astype(bf16)
        k_dec = (kk * jnp.exp(g_last - gc)).astype(bf16)
        vb = zi.astype(bf16)
        dec = jnp.exp(jnp.concatenate([g_last[:, h * B_DK:(h + 1) * B_DK] for h in range(B_HEADS)], axis=0))
        dec_t = dec.T
        gate = zg * _sigmoid(zg)
        for h in range(B_HEADS):
            sl = slice(h * B_DK, (h + 1) * B_DK)
            s_prev = s_ref[si, h]
            o = jnp.dot(q_dec[:, sl], s_prev.astype(bf16), preferred_element_type=f32)
            qh, gh, kh, vh = qs[:, sl], gc2[:, sl], kk[:, sl], zi[:, sl]
            q_tiles = [qh[SEG * r:SEG * (r + 1), :] for r in range(n_tiles)]
            g_tiles = [gh[SEG * r:SEG * (r + 1), :] for r in range(n_tiles)]
            terms = []
            for s in range(c):
                gs, ks = gh[s:s + 1, :], kh[s:s + 1, :]
                for r in range(s // SEG, n_tiles):
                    e = jnp.exp2(g_tiles[r] - gs)
                    if r == s // SEG:
                        e = jnp.where(rows1 >= s % SEG, e, 0.0)
                    terms.append(q_tiles[r] * (ks * e))
            a_rows = jnp.dot(jnp.concatenate(terms, axis=0).astype(bf16), ones, preferred_element_type=f32)
            o_tiles = [o[SEG * r:SEG * (r + 1), :] for r in range(n_tiles)]
            n = 0
            for s in range(c):
                for r in range(s // SEG, n_tiles):
                    o_tiles[r] = o_tiles[r] + a_rows[SEG * n:SEG * (n + 1), :] * vh[s:s + 1, :]
                    n += 1
            o = jnp.concatenate(o_tiles, axis=0)
            upd = lax.dot_general(k_dec[:, sl], vb[:, sl], (((0,), (0,)), ((), ())),
                                  preferred_element_type=f32)
            s_ref[si, h] = dec_t[:, h:h + 1] * s_prev + upd
            on = o * lax.rsqrt(jnp.mean(o * o, axis=-1, keepdims=True) + EPS) * hng[:, sl]
            yb_ref[pl.ds(r0, c), sl] = (on * gate[:, sl]).astype(bf16)

    n_chunks = tb // c
    for si in range(ns):
        if n_chunks == 1:
            chunk(si, si * tb)
        else:
            def body(ci, carry, si=si):
                chunk(si, pl.multiple_of(si * tb + ci * c, c))
                return carry
            lax.fori_loop(0, n_chunks, body, 0, unroll=2)


def _hgrn(z, s0, s0_layer, vecs, n_seq, t, row0, ns, tb, c, layer, states):
    rows = ns * tb
    tpb = t // tb
    base = row0 // rows
    grid = (n_seq // ns, tpb)

    def zspec(colblk):
        return pl.BlockSpec((rows, D_B), lambda n, j: (base + n * tpb + j, colblk))

    vspec = pl.BlockSpec((1, D_B), lambda n, j: (0, 0))
    ins = [z, z, z, z, s0, *vecs, states]
    specs = [zspec(2), zspec(3), zspec(4), zspec(5),
             pl.BlockSpec((None, ns, B_HEADS, B_DK, B_DK), lambda n, j: (s0_layer, n, 0, 0, 0)),
             vspec, vspec, vspec, vspec, pl.BlockSpec(memory_space=pl.ANY)]
    yb, states = pl.pallas_call(
        functools.partial(_hgrn_kernel, ns, tb, c),
        grid=grid,
        in_specs=specs,
        out_specs=[pl.BlockSpec((rows, D_B), lambda n, j: (n * tpb + j, 0)),
                   pl.BlockSpec((None, ns, B_HEADS, B_DK, B_DK), lambda n, j: (layer, n, 0, 0, 0))],
        out_shape=[jax.ShapeDtypeStruct((n_seq * t, D_B), bf16),
                   jax.ShapeDtypeStruct(states.shape, f32)],
        input_output_aliases={len(ins) - 1: 1},
        compiler_params=_params(("arbitrary", "arbitrary")),
        name="hgrn",
    )(*ins)
    return yb, states


def _merge_kernel(prompt_blocks, ya_ref, ybp_ref, ybs_ref, ga_ref, gb_ref, x_ref, g1p_ref, g1s_ref, wa_ref, wb_ref,
                  wo_ref, n2_ref, shp_ref, shs_ref, scp_ref, scs_ref, rw_ref, rb_ref, xo_ref, h2_ref, lg_ref):
    is_prompt = pl.program_id(0) < prompt_blocks
    gate1 = _pick_mod(prompt_blocks, g1p_ref, g1s_ref)
    scale2 = _pick_mod(prompt_blocks, scp_ref, scs_ref)
    shift2 = _pick_mod(prompt_blocks, shp_ref, shs_ref)
    d = D_MODEL
    segs = SEGS // MERGE_SPLIT
    rows = segs * SEG
    for part in range(MERGE_SPLIT):
        r = slice(part * rows, (part + 1) * rows)
        sg = slice(part * segs, (part + 1) * segs)
        a = jnp.dot(ya_ref[r, :], wa_ref[...], preferred_element_type=f32)
        yb = jnp.where(is_prompt, ybp_ref[r, :], ybs_ref[r, :])
        b = jnp.dot(yb, wb_ref[...], preferred_element_type=f32)
        merged = _sigmoid(ga_ref[r, :].astype(f32)) * a + _sigmoid(gb_ref[r, :].astype(f32)) * b
        mix = jnp.dot(merged.astype(bf16), wo_ref[...], preferred_element_type=f32)
        x = x_ref[sg] + gate1[sg] * mix.reshape(segs, SEG, d)
        xo_ref[sg] = x
        hn = x * lax.rsqrt(jnp.mean(x * x, axis=-1, keepdims=True) + EPS) * n2_ref[...]
        hn = (hn * (1.0 + scale2[sg]) + shift2[sg]).reshape(rows, d)
        h2_ref[r, :] = hn
        hi = hn.astype(bf16)
        lo = (hn - hi.astype(f32)).astype(bf16)
        hh_hl = jnp.dot(hi, rw_ref[...], preferred_element_type=f32)
        lh = jnp.dot(lo, rw_ref[:, :N_EXPERTS], preferred_element_type=f32)
        lg_ref[r, :] = hh_hl[:, :N_EXPERTS] + hh_hl[:, N_EXPERTS:] + lh + rb_ref[...]


def _merge(geo, ya, yb_p, yb_s, z, x3, mod, layer, wa, wb, wo, n2g, rw, rb):
    d = D_MODEL
    ga_blk = (2 * D_A + 4 * D_B) // d
    pb = geo.prompt_blocks

    def const(shape):
        return pl.BlockSpec(shape, lambda i: tuple(0 for _ in shape), pipeline_mode=pl.Buffered(1))

    rw_hi = rw.astype(bf16)
    rw_split = jnp.concatenate([rw_hi, (rw - rw_hi.astype(f32)).astype(bf16)], axis=1)
    ins = [ya, yb_p, yb_s, z, z, x3, mod, mod, wa, wb, wo, n2g.reshape(1, 1, d), mod, mod, mod, mod,
           rw_split, rb.reshape(1, N_EXPERTS)]
    specs = ([geo.row_spec(D_A),
              pl.BlockSpec((ROW_BLOCK, D_B), lambda i: (jnp.minimum(i, pb - 1), 0)),
              pl.BlockSpec((ROW_BLOCK, D_B), lambda i: (jnp.maximum(i - pb, 0), 0)),
              geo.row_spec(d, ga_blk), geo.row_spec(d, ga_blk + 1), geo.x_spec()]
             + geo.mod_specs(layer, 2)
             + [const((D_A, d)), const((D_B, d)), const((d, d)), const((1, 1, d))]
             + geo.mod_specs(layer, 3) + geo.mod_specs(layer, 4)
             + [const((d, 2 * N_EXPERTS)), const((1, N_EXPERTS))])
    return pl.pallas_call(
        functools.partial(_merge_kernel, geo.prompt_blocks),
        grid=(geo.n_blocks,),
        in_specs=specs,
        out_specs=[geo.x_spec(), geo.row_spec(d), geo.row_spec(N_EXPERTS)],
        out_shape=[jax.ShapeDtypeStruct(x3.shape, f32), jax.ShapeDtypeStruct((geo.rows, d), f32),
                   jax.ShapeDtypeStruct((geo.rows, N_EXPERTS), f32)],
        compiler_params=_params(("arbitrary",)),
        name="merge",
    )(*ins)


def _up_kernel(layer, nf, be_ref, first_ref, tix_ref, nxt_ref, meta_ref, xs_ref, w_hbm, bg_ref, bl_ref, o_ref,
               stage, wgb_ref, wlb_ref, sem):
    j, i = pl.program_id(0), pl.program_id(1)
    n_tiles = meta_ref[1]

    def tile_copies(e, jj, slot):
        return [pltpu.make_async_copy(
            w_hbm.at[layer, e, :, pl.ds(pl.multiple_of((half * nf + jj) * UP_TF, UP_TF), UP_TF)],
            stage.at[slot, half], sem.at[slot, half]) for half in range(2)]

    @pl.when(first_ref[i] == 1)
    def _():
        q = j * n_tiles + tix_ref[i]
        slot = q % 2

        @pl.when(q == 0)
        def _():
            for cp in tile_copies(be_ref[i], j, slot):
                cp.start()

        for cp in tile_copies(be_ref[i], j, slot):
            cp.wait()
        next_j = jnp.where(tix_ref[i] == n_tiles - 1, j + 1, j)

        @pl.when(next_j < nf)
        def _():
            for cp in tile_copies(nxt_ref[i], next_j, 1 - slot):
                cp.start()

    def finish(gate, lin):
        gate = jnp.minimum(gate, SWIGLU_LIMIT)
        lin = jnp.clip(lin, -SWIGLU_LIMIT, SWIGLU_LIMIT)
        glu = gate * _sigmoid(SWIGLU_ALPHA * gate)
        o_ref[...] = ((lin + 1.0) * glu).astype(bf16)

    @pl.when(first_ref[i] == 1)
    def _():
        slot = (j * n_tiles + tix_ref[i]) % 2
        x = xs_ref[...].astype(bf16)
        gate, lin = bg_ref[...], bl_ref[...]
        for kc in range(D_MODEL // CAST_K):
            ks = slice(kc * CAST_K, (kc + 1) * CAST_K)
            wg = stage[slot, 0, ks, :].astype(bf16)
            wl = stage[slot, 1, ks, :].astype(bf16)
            wgb_ref[ks, :] = wg
            wlb_ref[ks, :] = wl
            gate = gate + jnp.dot(x[:, ks], wg, preferred_element_type=f32)
            lin = lin + jnp.dot(x[:, ks], wl, preferred_element_type=f32)
        finish(gate, lin)

    @pl.when((first_ref[i] == 0) & (i < meta_ref[0]))
    def _():
        x = xs_ref[...].astype(bf16)
        finish(jnp.dot(x, wgb_ref[...], preferred_element_type=f32) + bg_ref[...],
               jnp.dot(x, wlb_ref[...], preferred_element_type=f32) + bl_ref[...])

    @pl.when(i >= meta_ref[0])
    def _():
        o_ref[...] = jnp.zeros_like(o_ref)


def _down_kernel(layer, half_blocks, be_ref, first_ref, tix_ref, nxt_ref, meta_ref, a1_ref, a2_ref, w_hbm, b_ref,
                 rw_ref, o_ref, stage, wb_ref, sem):
    i = pl.program_id(0)

    def act(cols=slice(None)):
        return jnp.where(i < half_blocks, a1_ref[:, cols], a2_ref[:, cols])

    def tile_copy(e, slot):
        return pltpu.make_async_copy(w_hbm.at[layer, e], stage.at[slot], sem.at[slot])

    @pl.when(first_ref[i] == 1)
    def _():
        q = tix_ref[i]
        slot = q % 2

        @pl.when(q == 0)
        def _():
            tile_copy(be_ref[i], slot).start()

        tile_copy(be_ref[i], slot).wait()

        @pl.when(q + 1 < meta_ref[1])
        def _():
            tile_copy(nxt_ref[i], 1 - slot).start()

    @pl.when(first_ref[i] == 1)
    def _():
        slot = tix_ref[i] % 2
        y = b_ref[...]
        for kc in range(D_FF // CAST_K):
            ks = slice(kc * CAST_K, (kc + 1) * CAST_K)
            w = stage[slot, ks, :].astype(bf16)
            wb_ref[ks, :] = w
            y = y + jnp.dot(act(ks), w, preferred_element_type=f32)
        o_ref[...] = y * rw_ref[...]

    @pl.when((first_ref[i] == 0) & (i < meta_ref[0]))
    def _():
        y = jnp.dot(act(), wb_ref[...], preferred_element_type=f32) + b_ref[...]
        o_ref[...] = y * rw_ref[...]

    @pl.when(i >= meta_ref[0])
    def _():
        o_ref[...] = jnp.zeros_like(o_ref)


def _expert_up(xs, sched, w_up, b_up, layer):
    n_rows = xs.shape[0]
    nb = n_rows // MOE_TM
    nf = D_FF // UP_TF
    return pl.pallas_call(
        functools.partial(_up_kernel, layer, nf),
        grid_spec=pltpu.PrefetchScalarGridSpec(
            num_scalar_prefetch=5,
            grid=(nf, nb),
            in_specs=[
                pl.BlockSpec((MOE_TM, D_MODEL), lambda j, i, be, *_: (i, 0)),
                pl.BlockSpec(memory_space=pl.ANY),
                pl.BlockSpec((None, None, 1, UP_TF), lambda j, i, be, *_: (layer, be[i], 0, j)),
                pl.BlockSpec((None, None, 1, UP_TF), lambda j, i, be, *_: (layer, be[i], 0, nf + j)),
            ],
            out_specs=pl.BlockSpec((MOE_TM, UP_TF), lambda j, i, be, *_: (i, j)),
            scratch_shapes=[pltpu.VMEM((2, 2, D_MODEL, UP_TF), f32),
                            pltpu.VMEM((D_MODEL, UP_TF), bf16), pltpu.VMEM((D_MODEL, UP_TF), bf16),
                            pltpu.SemaphoreType.DMA((2, 2))],
        ),
        out_shape=jax.ShapeDtypeStruct((n_rows, D_FF), bf16),
        compiler_params=_params(("arbitrary", "arbitrary")),
        name="moe_up",
    )(*sched, xs, w_up, b_up, b_up)


def _expert_down(acts, row_w, sched, w_down, b_down, layer):
    hb = acts[0].shape[0] // MOE_TM
    nb = 2 * hb
    n_rows = nb * MOE_TM
    return pl.pallas_call(
        functools.partial(_down_kernel, layer, hb),
        grid_spec=pltpu.PrefetchScalarGridSpec(
            num_scalar_prefetch=5,
            grid=(nb,),
            in_specs=[
                pl.BlockSpec((MOE_TM, D_FF), lambda i, be, *_: (jnp.minimum(i, hb - 1), 0)),
                pl.BlockSpec((MOE_TM, D_FF), lambda i, be, *_: (jnp.maximum(i - hb, 0), 0)),
                pl.BlockSpec(memory_space=pl.ANY),
                pl.BlockSpec((None, None, 1, D_MODEL), lambda i, be, *_: (layer, be[i], 0, 0)),
                pl.BlockSpec((MOE_TM, 1), lambda i, be, *_: (i, 0)),
            ],
            out_specs=pl.BlockSpec((MOE_TM, D_MODEL), lambda i, be, *_: (i, 0)),
            scratch_shapes=[pltpu.VMEM((2, D_FF, D_MODEL), f32), pltpu.VMEM((D_FF, D_MODEL), bf16),
                            pltpu.SemaphoreType.DMA((2,))],
        ),
        out_shape=jax.ShapeDtypeStruct((n_rows, D_MODEL), f32),
        compiler_params=_params(("arbitrary",)),
        name="moe_down",
    )(*sched, acts[0], acts[1], w_down, b_down, row_w)


def _tile_schedule(block_e, first, n_valid):
    n = block_e.shape[0]
    blk = jnp.arange(n, dtype=jnp.int32)
    tix = jnp.maximum(jnp.cumsum(first) - 1, 0).astype(jnp.int32)
    first_pos = jnp.where(first == 1, blk, n)
    next_pos = jnp.concatenate([lax.cummin(first_pos, reverse=True)[1:], jnp.full((1,), n, jnp.int32)])
    nxt = jnp.where(next_pos < n, block_e[jnp.minimum(next_pos, n - 1)], block_e[0]).astype(jnp.int32)
    meta = jnp.stack([n_valid.astype(jnp.int32), jnp.sum(first).astype(jnp.int32)])
    return (block_e, first.astype(jnp.int32), tix, nxt, meta)


def _route(logits):
    t = logits.shape[0]
    tk = t * TOP_K
    lane = jnp.arange(N_EXPERTS, dtype=jnp.int32)[None, :]
    rest, idx, val = logits, [], []
    for _ in range(TOP_K):
        i = jnp.argmax(rest, axis=1).astype(jnp.int32)
        idx.append(i)
        val.append(jnp.max(rest, axis=1))
        rest = jnp.where(lane == i[:, None], -jnp.inf, rest)
    top_e = jnp.stack(idx, axis=1)
    weights = jax.nn.softmax(jnp.stack(val, axis=1), axis=-1)
    member = sum((lane == i[:, None]).astype(jnp.int32) for i in idx)
    csum = jnp.cumsum(member, axis=0)
    counts = csum[-1]
    rank = csum - member
    padded = (counts + MOE_TM - 1) // MOE_TM * MOE_TM
    pad_end = jnp.cumsum(padded)
    pad_start = pad_end - padded
    start = jnp.cumsum(counts) - counts
    slot_row = pad_start[None, :] + rank
    pos = jnp.stack([jnp.sum(jnp.where(lane == i[:, None], slot_row, 0), axis=1) for i in idx], axis=1)
    n_blocks = tk // MOE_TM + N_EXPERTS
    n_rows = n_blocks * MOE_TM
    blk = jnp.arange(n_blocks, dtype=jnp.int32)
    blk_start = blk * MOE_TM
    block_e = jnp.minimum(jnp.sum(pad_end[None, :] <= blk_start[:, None], axis=1), N_EXPERTS - 1).astype(jnp.int32)
    n_valid = (pad_end[-1] // MOE_TM).astype(jnp.int32)
    first = ((blk_start == pad_start[block_e]) & (blk_start < pad_end[-1])).astype(jnp.int32)
    hb = n_blocks // 2
    halves = []
    for h in range(2):
        nv_h = jnp.clip(n_valid - h * hb, 0, hb)
        f_h = jnp.where((blk[:hb] == 0) & (nv_h > 0), 1, first[h * hb:(h + 1) * hb])
        halves.append(_tile_schedule(block_e[h * hb:(h + 1) * hb], f_h, nv_h))
    sched = (_tile_schedule(block_e, first, n_valid), halves)
    order = jnp.argsort(top_e.reshape(tk), stable=True).astype(jnp.int32)
    in_blk = jnp.arange(MOE_TM, dtype=jnp.int32)[None, :]
    offset = (blk_start - pad_start[block_e])[:, None] + in_blk
    live = (offset < counts[block_e][:, None]).reshape(n_rows)
    slot = order[jnp.clip(start[block_e][:, None] + offset, 0, tk - 1).reshape(n_rows)]
    row_tok = jnp.where(live, slot // TOP_K, 0)
    row_w = jnp.where(live, weights.reshape(tk)[slot], 0.0)
    return row_tok, row_w.reshape(n_rows, 1), sched, pos.astype(jnp.int32)


def kernel(x_prompt, x_sample, c_prompt, c_sample, state_hgrn, w_mod, b_mod, norm1_g, w_in, gmlp_ln_g, gmlp_ln_b, gmlp_w_s, gmlp_b_s, hgrn_lb_raw, hgrn_norm_g, w_branch_a, w_branch_b, w_out, norm2_g, router_w, router_b, exp_w_up, exp_b_up, exp_w_down, exp_b_down, final_norm_g):
    n_p, t_p, d = x_prompt.shape
    n_s, t_s, _ = x_sample.shape
    rows_p, rows_s = n_p * t_p, n_s * t_s
    geo = _Rows(n_p, t_p, n_s, t_s)

    n_c = n_p + n_s
    n_c_pad = -(-n_c // 8) * 8
    c_all = jnp.concatenate([c_sample, c_prompt, jnp.zeros((n_c_pad - n_c, d), f32)], axis=0)
    mod = _modulation(c_all, w_mod, b_mod).reshape(DEPTH, n_c_pad, 1, N_MOD * d)

    p = jax.nn.softmax(hgrn_lb_raw.astype(f32), axis=0)
    lower = jnp.maximum(jnp.cumsum(p, axis=0) - p[0:1], 0.0)

    s0_prompt = jnp.zeros((1, n_p, B_HEADS, B_DK, B_DK), f32)
    states_p = jnp.zeros((DEPTH, n_p, B_HEADS, B_DK, B_DK), f32)
    states_s = jnp.zeros((DEPTH, n_s, B_HEADS, B_DK, B_DK), f32)
    b_up4 = exp_b_up.reshape(DEPTH, N_EXPERTS, 1, 2 * D_FF)
    b_down4 = exp_b_down.reshape(DEPTH, N_EXPERTS, 1, D_MODEL)
    reps = CHUNK // t_s

    x = jnp.concatenate([x_prompt.reshape(rows_p // SEG, SEG, d), x_sample], axis=0)
    moe_out = None
    v_rows = []
    for l in range(DEPTH):
        if l == 0:
            _, h = _norm(geo, x, norm1_g[l], mod=mod, mod_layer=l)
        else:
            x, h = _norm(geo, x, norm1_g[l], y=moe_out, mod=mod, res_layer=l - 1, mod_layer=l)
        z = _in_proj(h, w_in, l)

        ws = gmlp_w_s[l]
        w_mix = jnp.stack([ws, jnp.tile(ws[:, :t_s, :t_s], (1, reps, reps))])
        bs = gmlp_b_s[l]
        b_mix = jnp.stack([jnp.repeat(bs.T, A_GW, axis=1),
                           jnp.repeat(jnp.tile(bs[:, :t_s].T, (reps, 1)), A_GW, axis=1)])
        ya, vn = _gmlp(z, w_mix, b_mix, gmlp_ln_g[l].reshape(1, D_A), gmlp_ln_b[l].reshape(1, D_A), rows_p, t_s)
        v_rows.append(vn[rows_p:].reshape(n_s, t_s, D_A))

        lb = lower[l].reshape(1, D_B)
        vecs = (jnp.log(lb), jnp.log1p(-lb), 1.0 - lb, hgrn_norm_g[l].reshape(1, D_B).astype(f32))
        yb_p, states_p = _hgrn(z, s0_prompt, 0, vecs, n_p, t_p, 0, 1, HGRN_TB, HGRN_C, l, states_p)
        yb_s, states_s = _hgrn(z, state_hgrn, l, vecs, n_s, t_s, rows_p, HGRN_NS, t_s, t_s, l, states_s)

        wa, wb, wo = w_branch_a[l].astype(bf16), w_branch_b[l].astype(bf16), w_out[l].astype(bf16)
        x, h2, logits = _merge(geo, ya, yb_p, yb_s, z, x, mod, l, wa, wb, wo, norm2_g[l], router_w[l], router_b[l])

        row_tok, row_w, (sched, sched_halves), pos = _route(logits)
        half_rows = row_tok.shape[0] // 2
        acts = [_expert_up(h2[row_tok[h * half_rows:(h + 1) * half_rows]], sched_halves[h], exp_w_up, b_up4, l)
                for h in range(2)]
        y_rows = _expert_down(acts, row_w, sched, exp_w_down, b_down4, l)
        moe_out = y_rows[pos.T.reshape(-1)].reshape(TOP_K, geo.rows, d)

    _, y = _norm(geo, x, final_norm_g, y=moe_out, mod=mod, res_layer=DEPTH - 1, out_dtype=f32)
    return (y[:rows_p].reshape(n_p, t_p, d), y[rows_p:].reshape(n_s, t_s, d),
            states_p, states_s, jnp.stack(v_rows))
```

```python
import functools

import jax
import jax.numpy as jnp
from jax import lax
from jax.experimental import pallas as pl
from jax.experimental.pallas import tpu as pltpu

f32 = jnp.float32
bf16 = jnp.bfloat16

D_MODEL = 2048
DEPTH = 4
D_A = D_MODEL // 2
A_GROUPS = 8
A_GW = D_A // A_GROUPS
CHUNK = 128
D_B = D_MODEL // 2
B_DK = 128
B_HEADS = D_B // B_DK
IN_W = 2 * D_A + 4 * D_B + 2 * D_MODEL
N_EXPERTS = 32
TOP_K = 4
D_FF = D_MODEL
SWIGLU_LIMIT = 7.0
SWIGLU_ALPHA = 1.702
N_MOD = 6
EPS = 1e-6
LOG2_E = 1.4426950408889634

VMEM_LIMIT_BYTES = 56 * 1024 * 1024

SEG = 8
ROW_BLOCK = 256
SEGS = ROW_BLOCK // SEG
MERGE_SPLIT = 2
IN_TM = 1024
IN_TN = 1024
MOE_TM = 384
UP_TF = 1024
HGRN_TB = 256
HGRN_C = 16
HGRN_NS = 8


def _params(sem):
    return pltpu.CompilerParams(dimension_semantics=sem, vmem_limit_bytes=VMEM_LIMIT_BYTES)


def _sigmoid(x):
    return jax.nn.sigmoid(x)


def _gelu(x):
    return 0.5 * x * (1.0 + lax.erf(x * 0.7071067811865476))


def _mod_kernel(c_ref, w_ref, b_ref, o_ref):
    c = c_ref[...]
    a = (c * _sigmoid(c)).astype(bf16)
    o_ref[...] = jnp.dot(a, w_ref[...].astype(bf16), preferred_element_type=f32) + b_ref[...]


def _modulation(c_all, w_mod, b_mod):
    n = c_all.shape[0]
    tn = 1024
    return pl.pallas_call(
        _mod_kernel,
        grid=(DEPTH, N_MOD * D_MODEL // tn),
        in_specs=[
            pl.BlockSpec((n, D_MODEL), lambda l, j: (0, 0)),
            pl.BlockSpec((None, D_MODEL, tn), lambda l, j: (l, 0, j)),
            pl.BlockSpec((None, 1, tn), lambda l, j: (l, 0, j)),
        ],
        out_specs=pl.BlockSpec((None, n, tn), lambda l, j: (l, 0, j)),
        out_shape=jax.ShapeDtypeStruct((DEPTH, n, N_MOD * D_MODEL), f32),
        compiler_params=_params(("arbitrary", "arbitrary")),
        name="modulation",
    )(c_all, w_mod, b_mod.reshape(DEPTH, 1, N_MOD * D_MODEL))


class _Rows:
    def __init__(self, n_p, t_p, n_s, t_s):
        assert t_s == SEG and t_p % ROW_BLOCK == 0 and (n_s * t_s) % ROW_BLOCK == 0
        self.n_p, self.n_s = n_p, n_s
        self.blocks_per_seq = t_p // ROW_BLOCK
        self.prompt_blocks = n_p * self.blocks_per_seq
        self.n_blocks = self.prompt_blocks + n_s * t_s // ROW_BLOCK
        self.rows = self.n_blocks * ROW_BLOCK

    def x_spec(self):
        return pl.BlockSpec((SEGS, SEG, D_MODEL), lambda i: (i, 0, 0))

    def row_spec(self, width, colblk=0):
        return pl.BlockSpec((ROW_BLOCK, width), lambda i: (i, colblk))

    def mod_specs(self, layer, col):
        pb, bps, n_p, n_s = self.prompt_blocks, self.blocks_per_seq, self.n_p, self.n_s
        return [pl.BlockSpec((None, 1, 1, D_MODEL), lambda i: (layer, n_s + jnp.minimum(i // bps, n_p - 1), 0, col)),
                pl.BlockSpec((None, SEGS, 1, D_MODEL), lambda i: (layer, jnp.maximum(i - pb, 0), 0, col))]


def _pick_mod(prompt_blocks, p_ref, s_ref):
    return jnp.where(pl.program_id(0) < prompt_blocks, p_ref[...], s_ref[...])


def _norm_kernel(prompt_blocks, has_res, has_mod, *refs):
    refs = list(refs)
    x_ref = refs.pop(0)
    if has_res:
        y_ref, gp_ref, gs_ref = refs.pop(0), refs.pop(0), refs.pop(0)
    g_ref = refs.pop(0)
    if has_mod:
        shp_ref, shs_ref, scp_ref, scs_ref = (refs.pop(0) for _ in range(4))
    if has_res:
        xo_ref = refs.pop(0)
    h_ref = refs.pop(0)

    x = x_ref[...]
    nb, tb, d = x.shape
    if has_res:
        y = y_ref[0]
        for k in range(1, y_ref.shape[0]):
            y = y + y_ref[k]
        x = x + _pick_mod(prompt_blocks, gp_ref, gs_ref) * y.reshape(nb, tb, d)
        xo_ref[...] = x
    hn = x * lax.rsqrt(jnp.mean(x * x, axis=-1, keepdims=True) + EPS) * g_ref[...]
    if has_mod:
        hn = hn * (1.0 + _pick_mod(prompt_blocks, scp_ref, scs_ref)) + _pick_mod(prompt_blocks, shp_ref, shs_ref)
    h_ref[...] = hn.reshape(nb * tb, d).astype(h_ref.dtype)


def _norm(geo, x3, g, *, y=None, mod=None, res_layer=None, mod_layer=None, out_dtype=bf16):
    d = D_MODEL
    has_res, has_mod = y is not None, mod_layer is not None
    ins, specs = [x3], [geo.x_spec()]
    if has_res:
        ins += [y, mod, mod]
        specs += [pl.BlockSpec((y.shape[0], ROW_BLOCK, d), lambda i: (0, i, 0))] + geo.mod_specs(res_layer, 5)
    ins.append(g.reshape(1, 1, d))
    specs.append(pl.BlockSpec((1, 1, d), lambda i: (0, 0, 0)))
    if has_mod:
        ins += [mod] * 4
        specs += geo.mod_specs(mod_layer, 0) + geo.mod_specs(mod_layer, 1)
    out_shapes, out_specs = [], []
    if has_res:
        out_shapes.append(jax.ShapeDtypeStruct(x3.shape, f32))
        out_specs.append(geo.x_spec())
    out_shapes.append(jax.ShapeDtypeStruct((geo.rows, d), out_dtype))
    out_specs.append(geo.row_spec(d))
    res = pl.pallas_call(
        functools.partial(_norm_kernel, geo.prompt_blocks, has_res, has_mod),
        grid=(geo.n_blocks,), in_specs=specs, out_specs=out_specs, out_shape=out_shapes,
        compiler_params=_params(("arbitrary",)),
        name="norm",
    )(*ins)
    return res if has_res else (None, res[0])


def _in_kernel(h_ref, w_ref, o_ref, wb_ref):
    @pl.when(pl.program_id(1) == 0)
    def _():
        wb_ref[...] = w_ref[...].astype(bf16)

    o_ref[...] = jnp.dot(h_ref[...], wb_ref[...], preferred_element_type=f32).astype(bf16)


def _in_proj(h, w_in, layer):
    rows = h.shape[0]
    return pl.pallas_call(
        _in_kernel,
        grid=(IN_W // IN_TN, rows // IN_TM),
        in_specs=[
            pl.BlockSpec((IN_TM, D_MODEL), lambda j, m: (m, 0)),
            pl.BlockSpec((None, D_MODEL, IN_TN), lambda j, m: (layer, 0, j)),
        ],
        out_specs=pl.BlockSpec((IN_TM, IN_TN), lambda j, m: (m, j)),
        out_shape=jax.ShapeDtypeStruct((rows, IN_W), bf16),
        scratch_shapes=[pltpu.VMEM((D_MODEL, IN_TN), bf16)],
        compiler_params=_params(("arbitrary", "arbitrary")),
        name="in_proj",
    )(h, w_in)


def _gmlp_kernel(n_prompt_chunks, sample_shift, u_ref, v_ref, w_ref, b_ref, lg_ref, lb_ref, ya_ref, vn_ref):
    c = pl.program_id(0)
    u = _gelu(u_ref[...].astype(f32))
    v = _gelu(v_ref[...].astype(f32))
    vc = v - jnp.mean(v, axis=-1, keepdims=True)
    vn = vc * lax.rsqrt(jnp.mean(vc * vc, axis=-1, keepdims=True) + EPS) * lg_ref[...] + lb_ref[...]
    vn_ref[...] = vn
    shift = jnp.where(c >= n_prompt_chunks, sample_shift, 7)
    row = lax.broadcasted_iota(jnp.int32, (CHUNK, CHUNK), 0)
    col = lax.broadcasted_iota(jnp.int32, (CHUNK, CHUNK), 1)
    mask = ((row >> shift) == (col >> shift)) & (col <= row)
    vnb = vn.astype(bf16)
    mixed = []
    for g in range(A_GROUPS):
        wg = jnp.where(mask, w_ref[g], 0.0).astype(bf16)
        mixed.append(jnp.dot(wg, vnb[:, g * A_GW:(g + 1) * A_GW], preferred_element_type=f32))
    mixed = jnp.concatenate(mixed, axis=1) + b_ref[...]
    ya_ref[...] = (u * mixed).astype(bf16)


def _gmlp(z, w_mix, b_mix, ln_g, ln_b, rows_p, t_s):
    rows = z.shape[0]
    ncp = rows_p // CHUNK
    return pl.pallas_call(
        functools.partial(_gmlp_kernel, ncp, t_s.bit_length() - 1),
        grid=(rows // CHUNK,),
        in_specs=[
            pl.BlockSpec((CHUNK, D_A), lambda c: (c, 0)),
            pl.BlockSpec((CHUNK, D_A), lambda c: (c, 1)),
            pl.BlockSpec((None, A_GROUPS, CHUNK, CHUNK), lambda c: (jnp.where(c >= ncp, 1, 0), 0, 0, 0)),
            pl.BlockSpec((None, CHUNK, D_A), lambda c: (jnp.where(c >= ncp, 1, 0), 0, 0)),
            pl.BlockSpec((1, D_A), lambda c: (0, 0)),
            pl.BlockSpec((1, D_A), lambda c: (0, 0)),
        ],
        out_specs=[pl.BlockSpec((CHUNK, D_A), lambda c: (c, 0)), pl.BlockSpec((CHUNK, D_A), lambda c: (c, 0))],
        out_shape=[jax.ShapeDtypeStruct((rows, D_A), bf16), jax.ShapeDtypeStruct((rows, D_A), f32)],
        compiler_params=_params(("arbitrary",)),
        name="gmlp",
    )(z, z, w_mix, b_mix, ln_g, ln_b)


def _hgrn_kernel(ns, tb, c, q_ref, f_ref, i_ref, g_ref, s0_ref, llb_ref, l1m_ref, oml_ref, hng_ref, all_ref,
                 yb_ref, s_ref):
    del all_ref

    @pl.when(pl.program_id(1) == 0)
    def _():
        s_ref[...] = s0_ref[...]

    row = lax.broadcasted_iota(jnp.int32, (c, c), 0)
    col = lax.broadcasted_iota(jnp.int32, (c, c), 1)
    tri = (col <= row).astype(f32)
    rows1 = lax.broadcasted_iota(jnp.int32, (c, 1), 0)
    ones = jnp.ones((B_DK, B_DK), bf16)
    llb, l1m, oml, hng = llb_ref[...], l1m_ref[...], oml_ref[...], hng_ref[...]

    def chunk(si, r0):
        zq = q_ref[pl.ds(r0, c), :].astype(f32)
        zf = f_ref[pl.ds(r0, c), :].astype(f32)
        zi = i_ref[pl.ds(r0, c), :].astype(f32)
        zg = g_ref[pl.ds(r0, c), :].astype(f32)
        qs = zq * _sigmoid(zq)
        a = l1m + jnp.minimum(zf, 0.0) - jnp.log1p(jnp.exp(-jnp.abs(zf)))
        logf = jnp.maximum(a, llb) + jnp.log1p(jnp.exp(-jnp.abs(a - llb)))
        kk = oml * _sigmoid(-zf)
        gc = jnp.dot(tri, logf, preferred_element_type=f32, precision=lax.Precision.HIGHEST)
        gc2 = gc * LOG2_E
        g_last = gc[c - 1:c, :]
        q_dec = (qs * jnp.exp(gc)).astype(bf16)
        k_dec = (kk * jnp.exp(g_last - gc)).astype(bf16)
        vb = zi.astype(bf16)
        dec = jnp.exp(jnp.concatenate([g_last[:, h * B_DK:(h + 1) * B_DK] for h in range(B_HEADS)], axis=0))
        dec_t = dec.T
        gate = zg * _sigmoid(zg)
        for h in range(B_HEADS):
            sl = slice(h * B_DK, (h + 1) * B_DK)
            s_prev = s_ref[si, h]
            o = jnp.dot(q_dec[:, sl], s_prev.astype(bf16), preferred_element_type=f32)
            qh, gh, kh, vh = qs[:, sl], gc2[:, sl], kk[:, sl], zi[:, sl]
            terms = []
            for s in range(c):
                e = jnp.where(rows1 >= s, jnp.exp2(gh - gh[s:s + 1, :]), 0.0)
                terms.append(qh * (kh[s:s + 1, :] * e))
            a_rows = jnp.dot(jnp.concatenate(terms, axis=0).astype(bf16), ones, preferred_element_type=f32)
            for s in range(c):
                o = o + a_rows[s * c:(s + 1) * c, :] * vh[s:s + 1, :]
            upd = lax.dot_general(k_dec[:, sl], vb[:, sl], (((0,), (0,)), ((), ())),
                                  preferred_element_type=f32)
            s_ref[si, h] = dec_t[:, h:h + 1] * s_prev + upd
            on = o * lax.rsqrt(jnp.mean(o * o, axis=-1, keepdims=True) + EPS) * hng[:, sl]
            yb_ref[pl.ds(r0, c), sl] = (on * gate[:, sl]).astype(bf16)

    n_chunks = tb // c
    for si in range(ns):
        if n_chunks == 1:
            chunk(si, si * tb)
        else:
            def body(ci, carry, si=si):
                chunk(si, pl.multiple_of(si * tb + ci * c, c))
                return carry
            lax.fori_loop(0, n_chunks, body, 0, unroll=2)


def _hgrn(z, s0, s0_layer, vecs, n_seq, t, row0, ns, tb, c, layer, states):
    rows = ns * tb
    tpb = t // tb
    base = row0 // rows
    grid = (n_seq // ns, tpb)

    def zspec(colblk):
        return pl.BlockSpec((rows, D_B), lambda n, j: (base + n * tpb + j, colblk))

    vspec = pl.BlockSpec((1, D_B), lambda n, j: (0, 0))
    ins = [z, z, z, z, s0, *vecs, states]
    specs = [zspec(2), zspec(3), zspec(4), zspec(5),
             pl.BlockSpec((None, ns, B_HEADS, B_DK, B_DK), lambda n, j: (s0_layer, n, 0, 0, 0)),
             vspec, vspec, vspec, vspec, pl.BlockSpec(memory_space=pl.ANY)]
    yb, states = pl.pallas_call(
        functools.partial(_hgrn_kernel, ns, tb, c),
        grid=grid,
        in_specs=specs,
        out_specs=[pl.BlockSpec((rows, D_B), lambda n, j: (n * tpb + j, 0)),
                   pl.BlockSpec((None, ns, B_HEADS, B_DK, B_DK), lambda n, j: (layer, n, 0, 0, 0))],
        out_shape=[jax.ShapeDtypeStruct((n_seq * t, D_B), bf16),
                   jax.ShapeDtypeStruct(states.shape, f32)],
        input_output_aliases={len(ins) - 1: 1},
        compiler_params=_params(("arbitrary", "arbitrary")),
        name="hgrn",
    )(*ins)
    return yb, states


def _merge_kernel(prompt_blocks, ya_ref, ybp_ref, ybs_ref, ga_ref, gb_ref, x_ref, g1p_ref, g1s_ref, wa_ref, wb_ref,
                  wo_ref, n2_ref, shp_ref, shs_ref, scp_ref, scs_ref, rw_ref, rb_ref, xo_ref, h2_ref, lg_ref):
    is_prompt = pl.program_id(0) < prompt_blocks
    gate1 = _pick_mod(prompt_blocks, g1p_ref, g1s_ref)
    scale2 = _pick_mod(prompt_blocks, scp_ref, scs_ref)
    shift2 = _pick_mod(prompt_blocks, shp_ref, shs_ref)
    d = D_MODEL
    segs = SEGS // MERGE_SPLIT
    rows = segs * SEG
    for part in range(MERGE_SPLIT):
        r = slice(part * rows, (part + 1) * rows)
        sg = slice(part * segs, (part + 1) * segs)
        a = jnp.dot(ya_ref[r, :], wa_ref[...], preferred_element_type=f32)
        yb = jnp.where(is_prompt, ybp_ref[r, :], ybs_ref[r, :])
        b = jnp.dot(yb, wb_ref[...], preferred_element_type=f32)
        merged = _sigmoid(ga_ref[r, :].astype(f32)) * a + _sigmoid(gb_ref[r, :].astype(f32)) * b
        mix = jnp.dot(merged.astype(bf16), wo_ref[...], preferred_element_type=f32)
        x = x_ref[sg] + gate1[sg] * mix.reshape(segs, SEG, d)
        xo_ref[sg] = x
        hn = x * lax.rsqrt(jnp.mean(x * x, axis=-1, keepdims=True) + EPS) * n2_ref[...]
        hn = (hn * (1.0 + scale2[sg]) + shift2[sg]).reshape(rows, d)
        h2_ref[r, :] = hn
        hi = hn.astype(bf16)
        lo = (hn - hi.astype(f32)).astype(bf16)
        hh_hl = jnp.dot(hi, rw_ref[...], preferred_element_type=f32)
        lh = jnp.dot(lo, rw_ref[:, :N_EXPERTS], preferred_element_type=f32)
        lg_ref[r, :] = hh_hl[:, :N_EXPERTS] + hh_hl[:, N_EXPERTS:] + lh + rb_ref[...]


def _merge(geo, ya, yb_p, yb_s, z, x3, mod, layer, wa, wb, wo, n2g, rw, rb):
    d = D_MODEL
    ga_blk = (2 * D_A + 4 * D_B) // d
    pb = geo.prompt_blocks

    def const(shape):
        return pl.BlockSpec(shape, lambda i: tuple(0 for _ in shape), pipeline_mode=pl.Buffered(1))

    rw_hi = rw.astype(bf16)
    rw_split = jnp.concatenate([rw_hi, (rw - rw_hi.astype(f32)).astype(bf16)], axis=1)
    ins = [ya, yb_p, yb_s, z, z, x3, mod, mod, wa, wb, wo, n2g.reshape(1, 1, d), mod, mod, mod, mod,
           rw_split, rb.reshape(1, N_EXPERTS)]
    specs = ([geo.row_spec(D_A),
              pl.BlockSpec((ROW_BLOCK, D_B), lambda i: (jnp.minimum(i, pb - 1), 0)),
              pl.BlockSpec((ROW_BLOCK, D_B), lambda i: (jnp.maximum(i - pb, 0), 0)),
              geo.row_spec(d, ga_blk), geo.row_spec(d, ga_blk + 1), geo.x_spec()]
             + geo.mod_specs(layer, 2)
             + [const((D_A, d)), const((D_B, d)), const((d, d)), const((1, 1, d))]
             + geo.mod_specs(layer, 3) + geo.mod_specs(layer, 4)
             + [const((d, 2 * N_EXPERTS)), const((1, N_EXPERTS))])
    return pl.pallas_call(
        functools.partial(_merge_kernel, geo.prompt_blocks),
        grid=(geo.n_blocks,),
        in_specs=specs,
        out_specs=[geo.x_spec(), geo.row_spec(d), geo.row_spec(N_EXPERTS)],
        out_shape=[jax.ShapeDtypeStruct(x3.shape, f32), jax.ShapeDtypeStruct((geo.rows, d), f32),
                   jax.ShapeDtypeStruct((geo.rows, N_EXPERTS), f32)],
        compiler_params=_params(("arbitrary",)),
        name="merge",
    )(*ins)


def _up_kernel(layer, nf, be_ref, first_ref, tix_ref, nxt_ref, meta_ref, xs_ref, w_hbm, bg_ref, bl_ref, o_ref,
               stage, wgb_ref, wlb_ref, sem):
    j, i = pl.program_id(0), pl.program_id(1)
    n_tiles = meta_ref[1]

    def tile_copies(e, jj, slot):
        return [pltpu.make_async_copy(
            w_hbm.at[layer, e, :, pl.ds(pl.multiple_of((half * nf + jj) * UP_TF, UP_TF), UP_TF)],
            stage.at[slot, half], sem.at[slot, half]) for half in range(2)]

    @pl.when(first_ref[i] == 1)
    def _():
        q = j * n_tiles + tix_ref[i]
        slot = q % 2

        @pl.when(q == 0)
        def _():
            for cp in tile_copies(be_ref[i], j, slot):
                cp.start()

        for cp in tile_copies(be_ref[i], j, slot):
            cp.wait()
        next_j = jnp.where(tix_ref[i] == n_tiles - 1, j + 1, j)

        @pl.when(next_j < nf)
        def _():
            for cp in tile_copies(nxt_ref[i], next_j, 1 - slot):
                cp.start()

        wgb_ref[...] = stage[slot, 0].astype(bf16)
        wlb_ref[...] = stage[slot, 1].astype(bf16)

    @pl.when(i < meta_ref[0])
    def _():
        x = xs_ref[...].astype(bf16)
        gate = jnp.dot(x, wgb_ref[...], preferred_element_type=f32) + bg_ref[...]
        lin = jnp.dot(x, wlb_ref[...], preferred_element_type=f32) + bl_ref[...]
        gate = jnp.minimum(gate, SWIGLU_LIMIT)
        lin = jnp.clip(lin, -SWIGLU_LIMIT, SWIGLU_LIMIT)
        glu = gate * _sigmoid(SWIGLU_ALPHA * gate)
        o_ref[...] = ((lin + 1.0) * glu).astype(bf16)

    @pl.when(i >= meta_ref[0])
    def _():
        o_ref[...] = jnp.zeros_like(o_ref)


def _down_kernel(layer, be_ref, first_ref, tix_ref, nxt_ref, meta_ref, a_ref, w_hbm, b_ref, rw_ref, o_ref,
                 stage, wb_ref, sem):
    i = pl.program_id(0)

    def tile_copy(e, slot):
        return pltpu.make_async_copy(w_hbm.at[layer, e], stage.at[slot], sem.at[slot])

    @pl.when(first_ref[i] == 1)
    def _():
        q = tix_ref[i]
        slot = q % 2

        @pl.when(q == 0)
        def _():
            tile_copy(be_ref[i], slot).start()

        tile_copy(be_ref[i], slot).wait()

        @pl.when(q + 1 < meta_ref[1])
        def _():
            tile_copy(nxt_ref[i], 1 - slot).start()

        wb_ref[...] = stage[slot].astype(bf16)

    @pl.when(i < meta_ref[0])
    def _():
        y = jnp.dot(a_ref[...], wb_ref[...], preferred_element_type=f32) + b_ref[...]
        o_ref[...] = y * rw_ref[...]

    @pl.when(i >= meta_ref[0])
    def _():
        o_ref[...] = jnp.zeros_like(o_ref)


def _experts(xs, row_w, sched, w_up, b_up, w_down, b_down, layer):
    n_rows = xs.shape[0]
    nb = n_rows // MOE_TM
    nf = D_FF // UP_TF
    act = pl.pallas_call(
        functools.partial(_up_kernel, layer, nf),
        grid_spec=pltpu.PrefetchScalarGridSpec(
            num_scalar_prefetch=5,
            grid=(nf, nb),
            in_specs=[
                pl.BlockSpec((MOE_TM, D_MODEL), lambda j, i, be, *_: (i, 0)),
                pl.BlockSpec(memory_space=pl.ANY),
                pl.BlockSpec((None, None, 1, UP_TF), lambda j, i, be, *_: (layer, be[i], 0, j)),
                pl.BlockSpec((None, None, 1, UP_TF), lambda j, i, be, *_: (layer, be[i], 0, nf + j)),
            ],
            out_specs=pl.BlockSpec((MOE_TM, UP_TF), lambda j, i, be, *_: (i, j)),
            scratch_shapes=[pltpu.VMEM((2, 2, D_MODEL, UP_TF), f32),
                            pltpu.VMEM((D_MODEL, UP_TF), bf16), pltpu.VMEM((D_MODEL, UP_TF), bf16),
                            pltpu.SemaphoreType.DMA((2, 2))],
        ),
        out_shape=jax.ShapeDtypeStruct((n_rows, D_FF), bf16),
        compiler_params=_params(("arbitrary", "arbitrary")),
        name="moe_up",
    )(*sched, xs, w_up, b_up, b_up)
    return pl.pallas_call(
        functools.partial(_down_kernel, layer),
        grid_spec=pltpu.PrefetchScalarGridSpec(
            num_scalar_prefetch=5,
            grid=(nb,),
            in_specs=[
                pl.BlockSpec((MOE_TM, D_FF), lambda i, be, *_: (i, 0)),
                pl.BlockSpec(memory_space=pl.ANY),
                pl.BlockSpec((None, None, 1, D_MODEL), lambda i, be, *_: (layer, be[i], 0, 0)),
                pl.BlockSpec((MOE_TM, 1), lambda i, be, *_: (i, 0)),
            ],
            out_specs=pl.BlockSpec((MOE_TM, D_MODEL), lambda i, be, *_: (i, 0)),
            scratch_shapes=[pltpu.VMEM((2, D_FF, D_MODEL), f32), pltpu.VMEM((D_FF, D_MODEL), bf16),
                            pltpu.SemaphoreType.DMA((2,))],
        ),
        out_shape=jax.ShapeDtypeStruct((n_rows, D_MODEL), f32),
        compiler_params=_params(("arbitrary",)),
        name="moe_down",
    )(*sched, act, w_down, b_down, row_w)


def _route(logits):
    t = logits.shape[0]
    tk = t * TOP_K
    assert tk % MOE_TM == 0
    lane = jnp.arange(N_EXPERTS, dtype=jnp.int32)[None, :]
    rest, idx, val = logits, [], []
    for _ in range(TOP_K):
        i = jnp.argmax(rest, axis=1).astype(jnp.int32)
        idx.append(i)
        val.append(jnp.max(rest, axis=1))
        rest = jnp.where(lane == i[:, None], -jnp.inf, rest)
    top_e = jnp.stack(idx, axis=1)
    weights = jax.nn.softmax(jnp.stack(val, axis=1), axis=-1)
    member = sum((lane == i[:, None]).astype(jnp.int32) for i in idx)
    csum = jnp.cumsum(member, axis=0)
    counts = csum[-1]
    rank = csum - member
    padded = (counts + MOE_TM - 1) // MOE_TM * MOE_TM
    pad_end = jnp.cumsum(padded)
    pad_start = pad_end - padded
    start = jnp.cumsum(counts) - counts
    slot_row = pad_start[None, :] + rank
    pos = jnp.stack([jnp.sum(jnp.where(lane == i[:, None], slot_row, 0), axis=1) for i in idx], axis=1)
    n_blocks = tk // MOE_TM + N_EXPERTS
    n_rows = n_blocks * MOE_TM
    blk = jnp.arange(n_blocks, dtype=jnp.int32)
    blk_start = blk * MOE_TM
    block_e = jnp.minimum(jnp.sum(pad_end[None, :] <= blk_start[:, None], axis=1), N_EXPERTS - 1).astype(jnp.int32)
    n_valid = (pad_end[-1] // MOE_TM).astype(jnp.int32)
    first = ((blk_start == pad_start[block_e]) & (blk_start < pad_end[-1])).astype(jnp.int32)
    tix = jnp.maximum(jnp.cumsum(first) - 1, 0).astype(jnp.int32)
    first_pos = jnp.where(first == 1, blk, n_blocks)
    next_pos = jnp.concatenate([lax.cummin(first_pos, reverse=True)[1:], jnp.full((1,), n_blocks, jnp.int32)])
    nxt = jnp.where(next_pos < n_blocks, block_e[jnp.minimum(next_pos, n_blocks - 1)], block_e[0]).astype(jnp.int32)
    meta = jnp.stack([n_valid, jnp.sum(first).astype(jnp.int32)])
    order = jnp.argsort(top_e.reshape(tk), stable=True).astype(jnp.int32)
    in_blk = jnp.arange(MOE_TM, dtype=jnp.int32)[None, :]
    offset = (blk_start - pad_start[block_e])[:, None] + in_blk
    live = (offset < counts[block_e][:, None]).reshape(n_rows)
    slot = order[jnp.clip(start[block_e][:, None] + offset, 0, tk - 1).reshape(n_rows)]
    row_tok = jnp.where(live, slot // TOP_K, 0)
    row_w = jnp.where(live, weights.reshape(tk)[slot], 0.0)
    return row_tok, row_w.reshape(n_rows, 1), (block_e, first, tix, nxt, meta), pos.astype(jnp.int32)


def kernel(x_prompt, x_sample, c_prompt, c_sample, state_hgrn, w_mod, b_mod, norm1_g, w_in, gmlp_ln_g, gmlp_ln_b, gmlp_w_s, gmlp_b_s, hgrn_lb_raw, hgrn_norm_g, w_branch_a, w_branch_b, w_out, norm2_g, router_w, router_b, exp_w_up, exp_b_up, exp_w_down, exp_b_down, final_norm_g):
    n_p, t_p, d = x_prompt.shape
    n_s, t_s, _ = x_sample.shape
    rows_p, rows_s = n_p * t_p, n_s * t_s
    geo = _Rows(n_p, t_p, n_s, t_s)

    n_c = n_p + n_s
    n_c_pad = -(-n_c // 8) * 8
    c_all = jnp.concatenate([c_sample, c_prompt, jnp.zeros((n_c_pad - n_c, d), f32)], axis=0)
    mod = _modulation(c_all, w_mod, b_mod).reshape(DEPTH, n_c_pad, 1, N_MOD * d)

    p = jax.nn.softmax(hgrn_lb_raw.astype(f32), axis=0)
    lower = jnp.maximum(jnp.cumsum(p, axis=0) - p[0:1], 0.0)

    s0_prompt = jnp.zeros((1, n_p, B_HEADS, B_DK, B_DK), f32)
    states_p = jnp.zeros((DEPTH, n_p, B_HEADS, B_DK, B_DK), f32)
    states_s = jnp.zeros((DEPTH, n_s, B_HEADS, B_DK, B_DK), f32)
    b_up4 = exp_b_up.reshape(DEPTH, N_EXPERTS, 1, 2 * D_FF)
    b_down4 = exp_b_down.reshape(DEPTH, N_EXPERTS, 1, D_MODEL)
    reps = CHUNK // t_s

    x = jnp.concatenate([x_prompt.reshape(rows_p // SEG, SEG, d), x_sample], axis=0)
    moe_out = None
    v_rows = []
    for l in range(DEPTH):
        if l == 0:
            _, h = _norm(geo, x, norm1_g[l], mod=mod, mod_layer=l)
        else:
            x, h = _norm(geo, x, norm1_g[l], y=moe_out, mod=mod, res_layer=l - 1, mod_layer=l)
        z = _in_proj(h, w_in, l)

        ws = gmlp_w_s[l]
        w_mix = jnp.stack([ws, jnp.tile(ws[:, :t_s, :t_s], (1, reps, reps))])
        bs = gmlp_b_s[l]
        b_mix = jnp.stack([jnp.repeat(bs.T, A_GW, axis=1),
                           jnp.repeat(jnp.tile(bs[:, :t_s].T, (reps, 1)), A_GW, axis=1)])
        ya, vn = _gmlp(z, w_mix, b_mix, gmlp_ln_g[l].reshape(1, D_A), gmlp_ln_b[l].reshape(1, D_A), rows_p, t_s)
        v_rows.append(vn[rows_p:].reshape(n_s, t_s, D_A))

        lb = lower[l].reshape(1, D_B)
        vecs = (jnp.log(lb), jnp.log1p(-lb), 1.0 - lb, hgrn_norm_g[l].reshape(1, D_B).astype(f32))
        yb_p, states_p = _hgrn(z, s0_prompt, 0, vecs, n_p, t_p, 0, 1, HGRN_TB, HGRN_C, l, states_p)
        yb_s, states_s = _hgrn(z, state_hgrn, l, vecs, n_s, t_s, rows_p, HGRN_NS, t_s, t_s, l, states_s)

        wa, wb, wo = w_branch_a[l].astype(bf16), w_branch_b[l].astype(bf16), w_out[l].astype(bf16)
        x, h2, logits = _merge(geo, ya, yb_p, yb_s, z, x, mod, l, wa, wb, wo, norm2_g[l], router_w[l], router_b[l])

        row_tok, row_w, sched, pos = _route(logits)
        xs = h2[row_tok]
        y_rows = _experts(xs, row_w, sched, exp_w_up, b_up4, exp_w_down, b_down4, l)
        moe_out = y_rows[pos.T.reshape(-1)].reshape(TOP_K, geo.rows, d)

    _, y = _norm(geo, x, final_norm_g, y=moe_out, mod=mod, res_layer=DEPTH - 1, out_dtype=f32)
    return (y[:rows_p].reshape(n_p, t_p, d), y[rows_p:].reshape(n_s, t_s, d),
            states_p, states_s, jnp.stack(v_rows))
```

```python
import functools

import jax
import jax.numpy as jnp
from jax import lax
from jax.experimental import pallas as pl
from jax.experimental.pallas import tpu as pltpu

f32 = jnp.float32
bf16 = jnp.bfloat16

D_MODEL = 2048
DEPTH = 4
D_A = D_MODEL // 2
A_GROUPS = 8
A_GW = D_A // A_GROUPS
CHUNK = 128
D_B = D_MODEL // 2
B_DK = 128
B_HEADS = D_B // B_DK
IN_W = 2 * D_A + 4 * D_B + 2 * D_MODEL
N_EXPERTS = 32
TOP_K = 4
D_FF = D_MODEL
SWIGLU_LIMIT = 7.0
SWIGLU_ALPHA = 1.702
N_MOD = 6
EPS = 1e-6
LOG2_E = 1.4426950408889634

VMEM_LIMIT_BYTES = 56 * 1024 * 1024

SEG = 8
ROW_BLOCK = 256
SEGS = ROW_BLOCK // SEG
MERGE_SPLIT = 2
IN_TM = 1536
IN_TN = 1024
MOE_TM = 384
UP_TF = 1024
HGRN_TB = 256
HGRN_C = 16
HGRN_NS = 8


def _params(sem):
    return pltpu.CompilerParams(dimension_semantics=sem, vmem_limit_bytes=VMEM_LIMIT_BYTES)


def _sigmoid(x):
    return jax.nn.sigmoid(x)


def _gelu(x):
    return 0.5 * x * (1.0 + lax.erf(x * 0.7071067811865476))


def _mod_kernel(c_ref, w_ref, b_ref, o_ref):
    c = c_ref[...]
    a = (c * _sigmoid(c)).astype(bf16)
    o_ref[...] = jnp.dot(a, w_ref[...].astype(bf16), preferred_element_type=f32) + b_ref[...]


def _modulation(c_all, w_mod, b_mod):
    n = c_all.shape[0]
    tn = 1024
    return pl.pallas_call(
        _mod_kernel,
        grid=(DEPTH, N_MOD * D_MODEL // tn),
        in_specs=[
            pl.BlockSpec((n, D_MODEL), lambda l, j: (0, 0)),
            pl.BlockSpec((None, D_MODEL, tn), lambda l, j: (l, 0, j)),
            pl.BlockSpec((None, 1, tn), lambda l, j: (l, 0, j)),
        ],
        out_specs=pl.BlockSpec((None, n, tn), lambda l, j: (l, 0, j)),
        out_shape=jax.ShapeDtypeStruct((DEPTH, n, N_MOD * D_MODEL), f32),
        compiler_params=_params(("arbitrary", "arbitrary")),
        name="modulation",
    )(c_all, w_mod, b_mod.reshape(DEPTH, 1, N_MOD * D_MODEL))


class _Rows:
    def __init__(self, n_p, t_p, n_s, t_s):
        assert t_s == SEG and t_p % ROW_BLOCK == 0 and (n_s * t_s) % ROW_BLOCK == 0
        self.n_p, self.n_s = n_p, n_s
        self.blocks_per_seq = t_p // ROW_BLOCK
        self.prompt_blocks = n_p * self.blocks_per_seq
        self.n_blocks = self.prompt_blocks + n_s * t_s // ROW_BLOCK
        self.rows = self.n_blocks * ROW_BLOCK

    def x_spec(self):
        return pl.BlockSpec((SEGS, SEG, D_MODEL), lambda i: (i, 0, 0))

    def row_spec(self, width, colblk=0):
        return pl.BlockSpec((ROW_BLOCK, width), lambda i: (i, colblk))

    def mod_specs(self, layer, col):
        pb, bps, n_p, n_s = self.prompt_blocks, self.blocks_per_seq, self.n_p, self.n_s
        return [pl.BlockSpec((None, 1, 1, D_MODEL), lambda i: (layer, n_s + jnp.minimum(i // bps, n_p - 1), 0, col)),
                pl.BlockSpec((None, SEGS, 1, D_MODEL), lambda i: (layer, jnp.maximum(i - pb, 0), 0, col))]


def _pick_mod(prompt_blocks, p_ref, s_ref):
    return jnp.where(pl.program_id(0) < prompt_blocks, p_ref[...], s_ref[...])


def _norm_kernel(prompt_blocks, has_res, has_mod, *refs):
    refs = list(refs)
    x_ref = refs.pop(0)
    if has_res:
        y_ref, gp_ref, gs_ref = refs.pop(0), refs.pop(0), refs.pop(0)
    g_ref = refs.pop(0)
    if has_mod:
        shp_ref, shs_ref, scp_ref, scs_ref = (refs.pop(0) for _ in range(4))
    if has_res:
        xo_ref = refs.pop(0)
    h_ref = refs.pop(0)

    x = x_ref[...]
    nb, tb, d = x.shape
    if has_res:
        y = y_ref[0]
        for k in range(1, y_ref.shape[0]):
            y = y + y_ref[k]
        x = x + _pick_mod(prompt_blocks, gp_ref, gs_ref) * y.reshape(nb, tb, d)
        xo_ref[...] = x
    hn = x * lax.rsqrt(jnp.mean(x * x, axis=-1, keepdims=True) + EPS) * g_ref[...]
    if has_mod:
        hn = hn * (1.0 + _pick_mod(prompt_blocks, scp_ref, scs_ref)) + _pick_mod(prompt_blocks, shp_ref, shs_ref)
    h_ref[...] = hn.reshape(nb * tb, d).astype(h_ref.dtype)


def _norm(geo, x3, g, *, y=None, mod=None, res_layer=None, mod_layer=None, out_dtype=bf16):
    d = D_MODEL
    has_res, has_mod = y is not None, mod_layer is not None
    ins, specs = [x3], [geo.x_spec()]
    if has_res:
        ins += [y, mod, mod]
        specs += [pl.BlockSpec((y.shape[0], ROW_BLOCK, d), lambda i: (0, i, 0))] + geo.mod_specs(res_layer, 5)
    ins.append(g.reshape(1, 1, d))
    specs.append(pl.BlockSpec((1, 1, d), lambda i: (0, 0, 0)))
    if has_mod:
        ins += [mod] * 4
        specs += geo.mod_specs(mod_layer, 0) + geo.mod_specs(mod_layer, 1)
    out_shapes, out_specs = [], []
    if has_res:
        out_shapes.append(jax.ShapeDtypeStruct(x3.shape, f32))
        out_specs.append(geo.x_spec())
    out_shapes.append(jax.ShapeDtypeStruct((geo.rows, d), out_dtype))
    out_specs.append(geo.row_spec(d))
    res = pl.pallas_call(
        functools.partial(_norm_kernel, geo.prompt_blocks, has_res, has_mod),
        grid=(geo.n_blocks,), in_specs=specs, out_specs=out_specs, out_shape=out_shapes,
        compiler_params=_params(("arbitrary",)),
        name="norm",
    )(*ins)
    return res if has_res else (None, res[0])


def _in_kernel(h_ref, w_ref, o_ref, wb_ref):
    @pl.when(pl.program_id(1) == 0)
    def _():
        wb_ref[...] = w_ref[...].astype(bf16)

    o_ref[...] = jnp.dot(h_ref[...], wb_ref[...], preferred_element_type=f32).astype(bf16)


def _in_proj(h, w_in, layer):
    rows = h.shape[0]
    return pl.pallas_call(
        _in_kernel,
        grid=(IN_W // IN_TN, rows // IN_TM),
        in_specs=[
            pl.BlockSpec((IN_TM, D_MODEL), lambda j, m: (m, 0)),
            pl.BlockSpec((None, D_MODEL, IN_TN), lambda j, m: (layer, 0, j)),
        ],
        out_specs=pl.BlockSpec((IN_TM, IN_TN), lambda j, m: (m, j)),
        out_shape=jax.ShapeDtypeStruct((rows, IN_W), bf16),
        scratch_shapes=[pltpu.VMEM((D_MODEL, IN_TN), bf16)],
        compiler_params=_params(("arbitrary", "arbitrary")),
        name="in_proj",
    )(h, w_in)


def _gmlp_kernel(n_prompt_chunks, sample_shift, u_ref, v_ref, w_ref, b_ref, lg_ref, lb_ref, ya_ref, vn_ref):
    c = pl.program_id(0)
    u = _gelu(u_ref[...].astype(f32))
    v = _gelu(v_ref[...].astype(f32))
    vc = v - jnp.mean(v, axis=-1, keepdims=True)
    vn = vc * lax.rsqrt(jnp.mean(vc * vc, axis=-1, keepdims=True) + EPS) * lg_ref[...] + lb_ref[...]
    vn_ref[...] = vn
    shift = jnp.where(c >= n_prompt_chunks, sample_shift, 7)
    row = lax.broadcasted_iota(jnp.int32, (CHUNK, CHUNK), 0)
    col = lax.broadcasted_iota(jnp.int32, (CHUNK, CHUNK), 1)
    mask = ((row >> shift) == (col >> shift)) & (col <= row)
    vnb = vn.astype(bf16)
    mixed = []
    for g in range(A_GROUPS):
        wg = jnp.where(mask, w_ref[g], 0.0).astype(bf16)
        mixed.append(jnp.dot(wg, vnb[:, g * A_GW:(g + 1) * A_GW], preferred_element_type=f32))
    mixed = jnp.concatenate(mixed, axis=1) + b_ref[...]
    ya_ref[...] = (u * mixed).astype(bf16)


def _gmlp(z, w_mix, b_mix, ln_g, ln_b, rows_p, t_s):
    rows = z.shape[0]
    ncp = rows_p // CHUNK
    return pl.pallas_call(
        functools.partial(_gmlp_kernel, ncp, t_s.bit_length() - 1),
        grid=(rows // CHUNK,),
        in_specs=[
            pl.BlockSpec((CHUNK, D_A), lambda c: (c, 0)),
            pl.BlockSpec((CHUNK, D_A), lambda c: (c, 1)),
            pl.BlockSpec((None, A_GROUPS, CHUNK, CHUNK), lambda c: (jnp.where(c >= ncp, 1, 0), 0, 0, 0)),
            pl.BlockSpec((None, CHUNK, D_A), lambda c: (jnp.where(c >= ncp, 1, 0), 0, 0)),
            pl.BlockSpec((1, D_A), lambda c: (0, 0)),
            pl.BlockSpec((1, D_A), lambda c: (0, 0)),
        ],
        out_specs=[pl.BlockSpec((CHUNK, D_A), lambda c: (c, 0)), pl.BlockSpec((CHUNK, D_A), lambda c: (c, 0))],
        out_shape=[jax.ShapeDtypeStruct((rows, D_A), bf16), jax.ShapeDtypeStruct((rows, D_A), f32)],
        compiler_params=_params(("arbitrary",)),
        name="gmlp",
    )(z, z, w_mix, b_mix, ln_g, ln_b)


def _hgrn_kernel(ns, tb, c, q_ref, f_ref, i_ref, g_ref, s0_ref, llb_ref, l1m_ref, oml_ref, hng_ref, all_ref,
                 yb_ref, s_ref):
    del all_ref

    @pl.when(pl.program_id(1) == 0)
    def _():
        s_ref[...] = s0_ref[...]

    row = lax.broadcasted_iota(jnp.int32, (c, c), 0)
    col = lax.broadcasted_iota(jnp.int32, (c, c), 1)
    tri = (col <= row).astype(f32)
    rows1 = lax.broadcasted_iota(jnp.int32, (c, 1), 0)
    ones = jnp.ones((B_DK, B_DK), bf16)
    llb, l1m, oml, hng = llb_ref[...], l1m_ref[...], oml_ref[...], hng_ref[...]

    def chunk(si, r0):
        zq = q_ref[pl.ds(r0, c), :].astype(f32)
        zf = f_ref[pl.ds(r0, c), :].astype(f32)
        zi = i_ref[pl.ds(r0, c), :].astype(f32)
        zg = g_ref[pl.ds(r0, c), :].astype(f32)
        qs = zq * _sigmoid(zq)
        a = l1m + jnp.minimum(zf, 0.0) - jnp.log1p(jnp.exp(-jnp.abs(zf)))
        logf = jnp.maximum(a, llb) + jnp.log1p(jnp.exp(-jnp.abs(a - llb)))
        kk = oml * _sigmoid(-zf)
        gc = jnp.dot(tri, logf, preferred_element_type=f32, precision=lax.Precision.HIGHEST)
        gc2 = gc * LOG2_E
        g_last = gc[c - 1:c, :]
        q_dec = (qs * jnp.exp(gc)).astype(bf16)
        k_dec = (kk * jnp.exp(g_last - gc)).astype(bf16)
        vb = zi.astype(bf16)
        dec = jnp.exp(jnp.concatenate([g_last[:, h * B_DK:(h + 1) * B_DK] for h in range(B_HEADS)], axis=0))
        dec_t = dec.T
        gate = zg * _sigmoid(zg)
        for h in range(B_HEADS):
            sl = slice(h * B_DK, (h + 1) * B_DK)
            s_prev = s_ref[si, h]
            o = jnp.dot(q_dec[:, sl], s_prev.astype(bf16), preferred_element_type=f32)
            qh, gh, kh, vh = qs[:, sl], gc2[:, sl], kk[:, sl], zi[:, sl]
            terms = []
            for s in range(c):
                e = jnp.where(rows1 >= s, jnp.exp2(gh - gh[s:s + 1, :]), 0.0)
                terms.append(qh * (kh[s:s + 1, :] * e))
            a_rows = jnp.dot(jnp.concatenate(terms, axis=0).astype(bf16), ones, preferred_element_type=f32)
            for s in range(c):
                o = o + a_rows[s * c:(s + 1) * c, :] * vh[s:s + 1, :]
            upd = lax.dot_general(k_dec[:, sl], vb[:, sl], (((0,), (0,)), ((), ())),
                                  preferred_element_type=f32)
            s_ref[si, h] = dec_t[:, h:h + 1] * s_prev + upd
            on = o * lax.rsqrt(jnp.mean(o * o, axis=-1, keepdims=True) + EPS) * hng[:, sl]
            yb_ref[pl.ds(r0, c), sl] = (on * gate[:, sl]).astype(bf16)

    n_chunks = tb // c
    for si in range(ns):
        if n_chunks == 1:
            chunk(si, si * tb)
        else:
            def body(ci, carry, si=si):
                chunk(si, pl.multiple_of(si * tb + ci * c, c))
                return carry
            lax.fori_loop(0, n_chunks, body, 0, unroll=2)


def _hgrn(z, s0, s0_layer, vecs, n_seq, t, row0, ns, tb, c, layer, states):
    rows = ns * tb
    tpb = t // tb
    base = row0 // rows
    grid = (n_seq // ns, tpb)

    def zspec(colblk):
        return pl.BlockSpec((rows, D_B), lambda n, j: (base + n * tpb + j, colblk))

    vspec = pl.BlockSpec((1, D_B), lambda n, j: (0, 0))
    ins = [z, z, z, z, s0, *vecs, states]
    specs = [zspec(2), zspec(3), zspec(4), zspec(5),
             pl.BlockSpec((None, ns, B_HEADS, B_DK, B_DK), lambda n, j: (s0_layer, n, 0, 0, 0)),
             vspec, vspec, vspec, vspec, pl.BlockSpec(memory_space=pl.ANY)]
    yb, states = pl.pallas_call(
        functools.partial(_hgrn_kernel, ns, tb, c),
        grid=grid,
        in_specs=specs,
        out_specs=[pl.BlockSpec((rows, D_B), lambda n, j: (n * tpb + j, 0)),
                   pl.BlockSpec((None, ns, B_HEADS, B_DK, B_DK), lambda n, j: (layer, n, 0, 0, 0))],
        out_shape=[jax.ShapeDtypeStruct((n_seq * t, D_B), bf16),
                   jax.ShapeDtypeStruct(states.shape, f32)],
        input_output_aliases={len(ins) - 1: 1},
        compiler_params=_params(("arbitrary", "arbitrary")),
        name="hgrn",
    )(*ins)
    return yb, states


def _merge_kernel(prompt_blocks, ya_ref, ybp_ref, ybs_ref, ga_ref, gb_ref, x_ref, g1p_ref, g1s_ref, wa_ref, wb_ref,
                  wo_ref, n2_ref, shp_ref, shs_ref, scp_ref, scs_ref, rw_ref, rb_ref, xo_ref, h2_ref, lg_ref):
    is_prompt = pl.program_id(0) < prompt_blocks
    gate1 = _pick_mod(prompt_blocks, g1p_ref, g1s_ref)
    scale2 = _pick_mod(prompt_blocks, scp_ref, scs_ref)
    shift2 = _pick_mod(prompt_blocks, shp_ref, shs_ref)
    d = D_MODEL
    segs = SEGS // MERGE_SPLIT
    rows = segs * SEG
    for part in range(MERGE_SPLIT):
        r = slice(part * rows, (part + 1) * rows)
        sg = slice(part * segs, (part + 1) * segs)
        a = jnp.dot(ya_ref[r, :], wa_ref[...], preferred_element_type=f32)
        yb = jnp.where(is_prompt, ybp_ref[r, :], ybs_ref[r, :])
        b = jnp.dot(yb, wb_ref[...], preferred_element_type=f32)
        merged = _sigmoid(ga_ref[r, :].astype(f32)) * a + _sigmoid(gb_ref[r, :].astype(f32)) * b
        mix = jnp.dot(merged.astype(bf16), wo_ref[...], preferred_element_type=f32)
        x = x_ref[sg] + gate1[sg] * mix.reshape(segs, SEG, d)
        xo_ref[sg] = x
        hn = x * lax.rsqrt(jnp.mean(x * x, axis=-1, keepdims=True) + EPS) * n2_ref[...]
        hn = (hn * (1.0 + scale2[sg]) + shift2[sg]).reshape(rows, d)
        h2_ref[r, :] = hn
        hi = hn.astype(bf16)
        lo = (hn - hi.astype(f32)).astype(bf16)
        hh_hl = jnp.dot(hi, rw_ref[...], preferred_element_type=f32)
        lh = jnp.dot(lo, rw_ref[:, :N_EXPERTS], preferred_element_type=f32)
        lg_ref[r, :] = hh_hl[:, :N_EXPERTS] + hh_hl[:, N_EXPERTS:] + lh + rb_ref[...]


def _merge(geo, ya, yb_p, yb_s, z, x3, mod, layer, wa, wb, wo, n2g, rw, rb):
    d = D_MODEL
    ga_blk = (2 * D_A + 4 * D_B) // d
    pb = geo.prompt_blocks

    def const(shape):
        return pl.BlockSpec(shape, lambda i: tuple(0 for _ in shape), pipeline_mode=pl.Buffered(1))

    rw_hi = rw.astype(bf16)
    rw_split = jnp.concatenate([rw_hi, (rw - rw_hi.astype(f32)).astype(bf16)], axis=1)
    ins = [ya, yb_p, yb_s, z, z, x3, mod, mod, wa, wb, wo, n2g.reshape(1, 1, d), mod, mod, mod, mod,
           rw_split, rb.reshape(1, N_EXPERTS)]
    specs = ([geo.row_spec(D_A),
              pl.BlockSpec((ROW_BLOCK, D_B), lambda i: (jnp.minimum(i, pb - 1), 0)),
              pl.BlockSpec((ROW_BLOCK, D_B), lambda i: (jnp.maximum(i - pb, 0), 0)),
              geo.row_spec(d, ga_blk), geo.row_spec(d, ga_blk + 1), geo.x_spec()]
             + geo.mod_specs(layer, 2)
             + [const((D_A, d)), const((D_B, d)), const((d, d)), const((1, 1, d))]
             + geo.mod_specs(layer, 3) + geo.mod_specs(layer, 4)
             + [const((d, 2 * N_EXPERTS)), const((1, N_EXPERTS))])
    return pl.pallas_call(
        functools.partial(_merge_kernel, geo.prompt_blocks),
        grid=(geo.n_blocks,),
        in_specs=specs,
        out_specs=[geo.x_spec(), geo.row_spec(d), geo.row_spec(N_EXPERTS)],
        out_shape=[jax.ShapeDtypeStruct(x3.shape, f32), jax.ShapeDtypeStruct((geo.rows, d), f32),
                   jax.ShapeDtypeStruct((geo.rows, N_EXPERTS), f32)],
        compiler_params=_params(("arbitrary",)),
        name="merge",
    )(*ins)


def _up_kernel(layer, nf, be_ref, first_ref, tix_ref, nxt_ref, meta_ref, xs_ref, w_hbm, bg_ref, bl_ref, o_ref,
               stage, wgb_ref, wlb_ref, sem):
    j, i = pl.program_id(0), pl.program_id(1)
    n_tiles = meta_ref[1]

    def tile_copies(e, jj, slot):
        return [pltpu.make_async_copy(
            w_hbm.at[layer, e, :, pl.ds(pl.multiple_of((half * nf + jj) * UP_TF, UP_TF), UP_TF)],
            stage.at[slot, half], sem.at[slot, half]) for half in range(2)]

    @pl.when(first_ref[i] == 1)
    def _():
        q = j * n_tiles + tix_ref[i]
        slot = q % 2

        @pl.when(q == 0)
        def _():
            for cp in tile_copies(be_ref[i], j, slot):
                cp.start()

        for cp in tile_copies(be_ref[i], j, slot):
            cp.wait()
        next_j = jnp.where(tix_ref[i] == n_tiles - 1, j + 1, j)

        @pl.when(next_j < nf)
        def _():
            for cp in tile_copies(nxt_ref[i], next_j, 1 - slot):
                cp.start()

        wgb_ref[...] = stage[slot, 0].astype(bf16)
        wlb_ref[...] = stage[slot, 1].astype(bf16)

    @pl.when(i < meta_ref[0])
    def _():
        x = xs_ref[...].astype(bf16)
        gate = jnp.dot(x, wgb_ref[...], preferred_element_type=f32) + bg_ref[...]
        lin = jnp.dot(x, wlb_ref[...], preferred_element_type=f32) + bl_ref[...]
        gate = jnp.minimum(gate, SWIGLU_LIMIT)
        lin = jnp.clip(lin, -SWIGLU_LIMIT, SWIGLU_LIMIT)
        glu = gate * _sigmoid(SWIGLU_ALPHA * gate)
        o_ref[...] = ((lin + 1.0) * glu).astype(bf16)

    @pl.when(i >= meta_ref[0])
    def _():
        o_ref[...] = jnp.zeros_like(o_ref)


def _down_kernel(layer, be_ref, first_ref, tix_ref, nxt_ref, meta_ref, a_ref, w_hbm, b_ref, rw_ref, o_ref,
                 stage, wb_ref, sem):
    i = pl.program_id(0)

    def tile_copy(e, slot):
        return pltpu.make_async_copy(w_hbm.at[layer, e], stage.at[slot], sem.at[slot])

    @pl.when(first_ref[i] == 1)
    def _():
        q = tix_ref[i]
        slot = q % 2

        @pl.when(q == 0)
        def _():
            tile_copy(be_ref[i], slot).start()

        tile_copy(be_ref[i], slot).wait()

        @pl.when(q + 1 < meta_ref[1])
        def _():
            tile_copy(nxt_ref[i], 1 - slot).start()

        wb_ref[...] = stage[slot].astype(bf16)

    @pl.when(i < meta_ref[0])
    def _():
        y = jnp.dot(a_ref[...], wb_ref[...], preferred_element_type=f32) + b_ref[...]
        o_ref[...] = y * rw_ref[...]

    @pl.when(i >= meta_ref[0])
    def _():
        o_ref[...] = jnp.zeros_like(o_ref)


def _experts(xs, row_w, sched, w_up, b_up, w_down, b_down, layer):
    n_rows = xs.shape[0]
    nb = n_rows // MOE_TM
    nf = D_FF // UP_TF
    act = pl.pallas_call(
        functools.partial(_up_kernel, layer, nf),
        grid_spec=pltpu.PrefetchScalarGridSpec(
            num_scalar_prefetch=5,
            grid=(nf, nb),
            in_specs=[
                pl.BlockSpec((MOE_TM, D_MODEL), lambda j, i, be, *_: (i, 0)),
                pl.BlockSpec(memory_space=pl.ANY),
                pl.BlockSpec((None, None, 1, UP_TF), lambda j, i, be, *_: (layer, be[i], 0, j)),
                pl.BlockSpec((None, None, 1, UP_TF), lambda j, i, be, *_: (layer, be[i], 0, nf + j)),
            ],
            out_specs=pl.BlockSpec((MOE_TM, UP_TF), lambda j, i, be, *_: (i, j)),
            scratch_shapes=[pltpu.VMEM((2, 2, D_MODEL, UP_TF), f32),
                            pltpu.VMEM((D_MODEL, UP_TF), bf16), pltpu.VMEM((D_MODEL, UP_TF), bf16),
                            pltpu.SemaphoreType.DMA((2, 2))],
        ),
        out_shape=jax.ShapeDtypeStruct((n_rows, D_FF), bf16),
        compiler_params=_params(("arbitrary", "arbitrary")),
        name="moe_up",
    )(*sched, xs, w_up, b_up, b_up)
    return pl.pallas_call(
        functools.partial(_down_kernel, layer),
        grid_spec=pltpu.PrefetchScalarGridSpec(
            num_scalar_prefetch=5,
            grid=(nb,),
            in_specs=[
                pl.BlockSpec((MOE_TM, D_FF), lambda i, be, *_: (i, 0)),
                pl.BlockSpec(memory_space=pl.ANY),
                pl.BlockSpec((None, None, 1, D_MODEL), lambda i, be, *_: (layer, be[i], 0, 0)),
                pl.BlockSpec((MOE_TM, 1), lambda i, be, *_: (i, 0)),
            ],
            out_specs=pl.BlockSpec((MOE_TM, D_MODEL), lambda i, be, *_: (i, 0)),
            scratch_shapes=[pltpu.VMEM((2, D_FF, D_MODEL), f32), pltpu.VMEM((D_FF, D_MODEL), bf16),
                            pltpu.SemaphoreType.DMA((2,))],
        ),
        out_shape=jax.ShapeDtypeStruct((n_rows, D_MODEL), f32),
        compiler_params=_params(("arbitrary",)),
        name="moe_down",
    )(*sched, act, w_down, b_down, row_w)


def _route(logits):
    t = logits.shape[0]
    tk = t * TOP_K
    assert tk % MOE_TM == 0
    lane = jnp.arange(N_EXPERTS, dtype=jnp.int32)[None, :]
    rest, idx, val = logits, [], []
    for _ in range(TOP_K):
        i = jnp.argmax(rest, axis=1).astype(jnp.int32)
        idx.append(i)
        val.append(jnp.max(rest, axis=1))
        rest = jnp.where(lane == i[:, None], -jnp.inf, rest)
    top_e = jnp.stack(idx, axis=1)
    weights = jax.nn.softmax(jnp.stack(val, axis=1), axis=-1)
    member = sum((lane == i[:, None]).astype(jnp.int32) for i in idx)
    csum = jnp.cumsum(member, axis=0)
    counts = csum[-1]
    rank = csum - member
    padded = (counts + MOE_TM - 1) // MOE_TM * MOE_TM
    pad_end = jnp.cumsum(padded)
    pad_start = pad_end - padded
    start = jnp.cumsum(counts) - counts
    slot_row = pad_start[None, :] + rank
    pos = jnp.stack([jnp.sum(jnp.where(lane == i[:, None], slot_row, 0), axis=1) for i in idx], axis=1)
    n_blocks = tk // MOE_TM + N_EXPERTS
    n_rows = n_blocks * MOE_TM
    blk = jnp.arange(n_blocks, dtype=jnp.int32)
    blk_start = blk * MOE_TM
    block_e = jnp.minimum(jnp.sum(pad_end[None, :] <= blk_start[:, None], axis=1), N_EXPERTS - 1).astype(jnp.int32)
    n_valid = (pad_end[-1] // MOE_TM).astype(jnp.int32)
    first = ((blk_start == pad_start[block_e]) & (blk_start < pad_end[-1])).astype(jnp.int32)
    tix = jnp.maximum(jnp.cumsum(first) - 1, 0).astype(jnp.int32)
    first_pos = jnp.where(first == 1, blk, n_blocks)
    next_pos = jnp.concatenate([lax.cummin(first_pos, reverse=True)[1:], jnp.full((1,), n_blocks, jnp.int32)])
    nxt = jnp.where(next_pos < n_blocks, block_e[jnp.minimum(next_pos, n_blocks - 1)], block_e[0]).astype(jnp.int32)
    meta = jnp.stack([n_valid, jnp.sum(first).astype(jnp.int32)])
    order = jnp.argsort(top_e.reshape(tk), stable=True).astype(jnp.int32)
    in_blk = jnp.arange(MOE_TM, dtype=jnp.int32)[None, :]
    offset = (blk_start - pad_start[block_e])[:, None] + in_blk
    live = (offset < counts[block_e][:, None]).reshape(n_rows)
    slot = order[jnp.clip(start[block_e][:, None] + offset, 0, tk - 1).reshape(n_rows)]
    row_tok = jnp.where(live, slot // TOP_K, 0)
    row_w = jnp.where(live, weights.reshape(tk)[slot], 0.0)
    return row_tok, row_w.reshape(n_rows, 1), (block_e, first, tix, nxt, meta), pos.astype(jnp.int32)


def kernel(x_prompt, x_sample, c_prompt, c_sample, state_hgrn, w_mod, b_mod, norm1_g, w_in, gmlp_ln_g, gmlp_ln_b, gmlp_w_s, gmlp_b_s, hgrn_lb_raw, hgrn_norm_g, w_branch_a, w_branch_b, w_out, norm2_g, router_w, router_b, exp_w_up, exp_b_up, exp_w_down, exp_b_down, final_norm_g):
    n_p, t_p, d = x_prompt.shape
    n_s, t_s, _ = x_sample.shape
    rows_p, rows_s = n_p * t_p, n_s * t_s
    geo = _Rows(n_p, t_p, n_s, t_s)

    n_c = n_p + n_s
    n_c_pad = -(-n_c // 8) * 8
    c_all = jnp.concatenate([c_sample, c_prompt, jnp.zeros((n_c_pad - n_c, d), f32)], axis=0)
    mod = _modulation(c_all, w_mod, b_mod).reshape(DEPTH, n_c_pad, 1, N_MOD * d)

    p = jax.nn.softmax(hgrn_lb_raw.astype(f32), axis=0)
    lower = jnp.maximum(jnp.cumsum(p, axis=0) - p[0:1], 0.0)

    s0_prompt = jnp.zeros((1, n_p, B_HEADS, B_DK, B_DK), f32)
    states_p = jnp.zeros((DEPTH, n_p, B_HEADS, B_DK, B_DK), f32)
    states_s = jnp.zeros((DEPTH, n_s, B_HEADS, B_DK, B_DK), f32)
    b_up4 = exp_b_up.reshape(DEPTH, N_EXPERTS, 1, 2 * D_FF)
    b_down4 = exp_b_down.reshape(DEPTH, N_EXPERTS, 1, D_MODEL)
    reps = CHUNK // t_s

    x = jnp.concatenate([x_prompt.reshape(rows_p // SEG, SEG, d), x_sample], axis=0)
    moe_out = None
    v_rows = []
    for l in range(DEPTH):
        if l == 0:
            _, h = _norm(geo, x, norm1_g[l], mod=mod, mod_layer=l)
        else:
            x, h = _norm(geo, x, norm1_g[l], y=moe_out, mod=mod, res_layer=l - 1, mod_layer=l)
        z = _in_proj(h, w_in, l)

        ws = gmlp_w_s[l]
        w_mix = jnp.stack([ws, jnp.tile(ws[:, :t_s, :t_s], (1, reps, reps))])
        bs = gmlp_b_s[l]
        b_mix = jnp.stack([jnp.repeat(bs.T, A_GW, axis=1),
                           jnp.repeat(jnp.tile(bs[:, :t_s].T, (reps, 1)), A_GW, axis=1)])
        ya, vn = _gmlp(z, w_mix, b_mix, gmlp_ln_g[l].reshape(1, D_A), gmlp_ln_b[l].reshape(1, D_A), rows_p, t_s)
        v_rows.append(vn[rows_p:].reshape(n_s, t_s, D_A))

        lb = lower[l].reshape(1, D_B)
        vecs = (jnp.log(lb), jnp.log1p(-lb), 1.0 - lb, hgrn_norm_g[l].reshape(1, D_B).astype(f32))
        yb_p, states_p = _hgrn(z, s0_prompt, 0, vecs, n_p, t_p, 0, 1, HGRN_TB, HGRN_C, l, states_p)
        yb_s, states_s = _hgrn(z, state_hgrn, l, vecs, n_s, t_s, rows_p, HGRN_NS, t_s, t_s, l, states_s)

        wa, wb, wo = w_branch_a[l].astype(bf16), w_branch_b[l].astype(bf16), w_out[l].astype(bf16)
        x, h2, logits = _merge(geo, ya, yb_p, yb_s, z, x, mod, l, wa, wb, wo, norm2_g[l], router_w[l], router_b[l])

        row_tok, row_w, sched, pos = _route(logits)
        xs = h2[row_tok]
        y_rows = _experts(xs, row_w, sched, exp_w_up, b_up4, exp_w_down, b_down4, l)
        moe_out = y_rows[pos.T.reshape(-1)].reshape(TOP_K, geo.rows, d)

    _, y = _norm(geo, x, final_norm_g, y=moe_out, mod=mod, res_layer=DEPTH - 1, out_dtype=f32)
    return (y[:rows_p].reshape(n_p, t_p, d), y[rows_p:].reshape(n_s, t_s, d),
            states_p, states_s, jnp.stack(v_rows))
```
